```python
import jax, jax.numpy as jnp
from jax import lax
import numpy as np

D_MODEL = 1024
BATCH = 8
SEQ = 4096
DEPTH = 4
DEC_BATCH = 32
DEC_SEQ = 32
PAST_LEN = 4096

CHUNK = 64
N_MIXERS = 3
EXPAND = 2
D_INNER = EXPAND * D_MODEL
POOL_WINDOWS = (2, 4, 8, 16)
N_POOL_GROUPS = len(POOL_WINDOWS)
POOL_GROUP = D_INNER // N_POOL_GROUPS
POOL_HIST = max(POOL_WINDOWS) - 1
CONV_WIDTH = 31
CONV_HIST = CONV_WIDTH - 1
SGU_CHUNK = 128
SGU_HEADS = 4
SGU_HEAD_DIM = D_INNER // SGU_HEADS
RMS_EPS = 1e-6
LN_EPS = 1e-5
N_POOL = (DEPTH + 2) // 3
N_CONV = (DEPTH + 1) // 3
N_SGU = DEPTH // 3

kernel_name = 'hybrid_pool_conv_sgu_streaming_step'


def rmsnorm(x, g):
    xf = x.astype(jnp.float32)
    y = xf * lax.rsqrt(jnp.mean(xf * xf, axis=-1, keepdims=True) + RMS_EPS)
    return (y * g.astype(jnp.float32)).astype(x.dtype)


def layernorm(x, g, b):
    xf = x.astype(jnp.float32)
    mu = jnp.mean(xf, axis=-1, keepdims=True)
    var = jnp.mean(jnp.square(xf - mu), axis=-1, keepdims=True)
    y = (xf - mu) * lax.rsqrt(var + LN_EPS) * g.astype(jnp.float32) + b.astype(jnp.float32)
    return y.astype(x.dtype)


def pool_mixer(u, hist, offset, w_grp, scale):
    B, T, E = u.shape
    h = jnp.concatenate([hist.astype(u.dtype), u], axis=1).astype(jnp.float32)
    cs = jnp.concatenate([jnp.zeros_like(h[:, :1]), jnp.cumsum(h, axis=1)], axis=1)
    pos = offset + jnp.arange(T)
    means = []
    for g, w in enumerate(POOL_WINDOWS):
        c0, c1 = g * POOL_GROUP, (g + 1) * POOL_GROUP
        s = cs[:, POOL_HIST + 1:, c0:c1] - cs[:, POOL_HIST + 1 - w:POOL_HIST + 1 - w + T, c0:c1]
        cnt = jnp.minimum(pos + 1, w).astype(jnp.float32)[None, :, None]
        means.append(s / cnt)
    d = (jnp.concatenate(means, axis=-1) - u.astype(jnp.float32)).astype(u.dtype)
    d = d.reshape(B, T, N_POOL_GROUPS, POOL_GROUP)
    y = jnp.einsum('btgc,gcd->btgd', d, w_grp).reshape(B, T, E) * scale
    return y, h[:, -POOL_HIST:].astype(u.dtype)


def conv_mixer(a, hist, w, b, ng, nb):
    h = jnp.concatenate([hist.astype(a.dtype), a], axis=1)
    y = lax.conv_general_dilated(h, w[:, None, :], window_strides=(1,), padding='VALID',
                                 dimension_numbers=('NWC', 'WIO', 'NWC'),
                                 feature_group_count=D_INNER) + b
    y = jax.nn.silu(layernorm(y, ng, nb))
    return y, h[:, -CONV_HIST:]


def sgu_mixer(u, v, ws, bs, ng, nb):
    B, T, E = u.shape
    L = min(T, SGU_CHUNK)
    n = T // L
    vn = layernorm(v, ng, nb)
    mask = jnp.tril(jnp.ones((L, L), dtype=bool))
    wl = ws[:, :L, :L]
    wm = jnp.where(mask[None], wl, jnp.zeros_like(wl))
    vc = vn.reshape(B, n, L, SGU_HEADS, SGU_HEAD_DIM)
    s = jnp.einsum('hij,bnjhc->bnihc', wm, vc) + jnp.transpose(bs[:, :L])[:, :, None]
    y = u * s.reshape(B, T, E)
    return y, vn[:, -L:]


def run_trunk(x, pool_hist, conv_hist, offset, norm_g, pool_in_w, pool_w, pool_scale, pool_out_w,
              conv_in_w, conv_w, conv_b, conv_norm_g, conv_norm_b, conv_out_w,
              sgu_in_w, sgu_norm_g, sgu_norm_b, sgu_w, sgu_b, sgu_out_w, final_g):
    new_pool, new_conv, new_sgu = [], [], []
    for i in range(DEPTH):
        kind, j = i % N_MIXERS, i // N_MIXERS
        h = rmsnorm(x, norm_g[i])
        if kind == 0:
            p = h @ pool_in_w[j]
            u, z = p[..., :D_INNER], p[..., D_INNER:]
            y, st = pool_mixer(u, pool_hist[j], offset, pool_w[j], pool_scale[j])
            out = (y * jax.nn.silu(z)) @ pool_out_w[j]
            new_pool.append(st)
        elif kind == 1:
            p = h @ conv_in_w[j]
            a, gl, z = p[..., :D_INNER], p[..., D_INNER:2 * D_INNER], p[..., 2 * D_INNER:]
            y, st = conv_mixer(a * jax.nn.sigmoid(gl), conv_hist[j], conv_w[j], conv_b[j],
                               conv_norm_g[j], conv_norm_b[j])
            out = (y * jax.nn.silu(z)) @ conv_out_w[j]
            new_conv.append(st)
        else:
            p = h @ sgu_in_w[j]
            u, v, z = p[..., :D_INNER], p[..., D_INNER:2 * D_INNER], p[..., 2 * D_INNER:]
            y, st = sgu_mixer(u, v, sgu_w[j], sgu_b[j], sgu_norm_g[j], sgu_norm_b[j])
            out = (y * jax.nn.silu(z)) @ sgu_out_w[j]
            new_sgu.append(st)
        x = x + out
    return rmsnorm(x, final_g), jnp.stack(new_pool), jnp.stack(new_conv), jnp.stack(new_sgu)


def setup_inputs(seed: int = 0) -> dict:
    key = jax.random.key(seed)
    ks = jax.random.split(key, 24)
    nrm = jax.random.normal
    f32 = jnp.float32
    D, E = D_MODEL, D_INNER
    return {
        'x_prompt': nrm(ks[0], (BATCH, SEQ, D), f32),
        'x_sample': nrm(ks[1], (DEC_BATCH, DEC_SEQ, D), f32),
        'state_pool': nrm(ks[2], (N_POOL, DEC_BATCH, POOL_HIST, E), f32),
        'state_conv': 0.5 * nrm(ks[3], (N_CONV, DEC_BATCH, CONV_HIST, E), f32),
        'norm_g': 1.0 + 0.02 * nrm(ks[4], (DEPTH, D), f32),
        'pool_in_w': nrm(ks[5], (N_POOL, D, 2 * E), f32) * D ** -0.5,
        'pool_w': nrm(ks[6], (N_POOL, N_POOL_GROUPS, POOL_GROUP, POOL_GROUP), f32) * POOL_GROUP ** -0.5,
        'pool_scale': 1.0 + 0.02 * nrm(ks[7], (N_POOL, E), f32),
        'pool_out_w': nrm(ks[8], (N_POOL, E, D), f32) * E ** -0.5,
        'conv_in_w': nrm(ks[9], (N_CONV, D, 3 * E), f32) * D ** -0.5,
        'conv_w': nrm(ks[10], (N_CONV, CONV_WIDTH, E), f32) * CONV_WIDTH ** -0.5,
        'conv_b': 0.02 * nrm(ks[11], (N_CONV, E), f32),
        'conv_norm_g': 1.0 + 0.02 * nrm(ks[12], (N_CONV, E), f32),
        'conv_norm_b': 0.02 * nrm(ks[13], (N_CONV, E), f32),
        'conv_out_w': nrm(ks[14], (N_CONV, E, D), f32) * E ** -0.5,
        'sgu_in_w': nrm(ks[15], (N_SGU, D, 3 * E), f32) * D ** -0.5,
        'sgu_norm_g': 1.0 + 0.02 * nrm(ks[16], (N_SGU, E), f32),
        'sgu_norm_b': 0.02 * nrm(ks[17], (N_SGU, E), f32),
        'sgu_w': nrm(ks[18], (N_SGU, SGU_HEADS, SGU_CHUNK, SGU_CHUNK), f32) * SGU_CHUNK ** -0.5,
        'sgu_b': 1.0 + 0.02 * nrm(ks[19], (N_SGU, SGU_HEADS, SGU_CHUNK), f32),
        'sgu_out_w': nrm(ks[20], (N_SGU, E, D), f32) * E ** -0.5,
        'final_g': 1.0 + 0.02 * nrm(ks[21], (D,), f32),
    }


def reference(x_prompt, x_sample, state_pool, state_conv, norm_g, pool_in_w, pool_w, pool_scale, pool_out_w,
              conv_in_w, conv_w, conv_b, conv_norm_g, conv_norm_b, conv_out_w,
              sgu_in_w, sgu_norm_g, sgu_norm_b, sgu_w, sgu_b, sgu_out_w, final_g):
    b_p = x_prompt.shape[0]
    zero_pool = jnp.zeros((N_POOL, b_p, POOL_HIST, D_INNER), x_prompt.dtype)
    zero_conv = jnp.zeros((N_CONV, b_p, CONV_HIST, D_INNER), x_prompt.dtype)
    y_prompt, new_pool_prompt, new_conv_prompt, new_sgu_prompt = run_trunk(
        x_prompt, zero_pool, zero_conv, 0, norm_g, pool_in_w, pool_w, pool_scale, pool_out_w,
        conv_in_w, conv_w, conv_b, conv_norm_g, conv_norm_b, conv_out_w,
        sgu_in_w, sgu_norm_g, sgu_norm_b, sgu_w, sgu_b, sgu_out_w, final_g)
    y_sample, new_pool_sample, new_conv_sample, new_sgu_sample = run_trunk(
        x_sample, state_pool, state_conv, PAST_LEN, norm_g, pool_in_w, pool_w, pool_scale, pool_out_w,
        conv_in_w, conv_w, conv_b, conv_norm_g, conv_norm_b, conv_out_w,
        sgu_in_w, sgu_norm_g, sgu_norm_b, sgu_w, sgu_b, sgu_out_w, final_g)
    return (y_prompt, y_sample, new_pool_prompt, new_pool_sample, new_conv_prompt, new_conv_sample, new_sgu_prompt, new_sgu_sample)
```

```python
import functools

import jax
import jax.numpy as jnp
from jax import lax
from jax.experimental import pallas as pl
from jax.experimental.pallas import tpu as pltpu

POOL_WINDOWS = (2, 4, 8, 16)
POOL_HIST = max(POOL_WINDOWS) - 1
CONV_WIDTH = 31
CONV_HIST = CONV_WIDTH - 1
SGU_CHUNK = 128
SGU_HEADS = 4
RMS_EPS = 1e-6
LN_EPS = 1e-5

NSEQ = 8
POOL_PAD = 16
CONV_PAD = 32
TILE_STEPS = 32
SGU_TILE = 256
VMEM_LIMIT_BYTES = 56 * 1024 * 1024

F32 = jnp.float32
BF16 = jnp.bfloat16


def _silu(v):
    return v * jax.nn.sigmoid(v)


def _row_loop(n_chunks, body):
    def step(i, carry):
        body(i)
        return carry
    lax.fori_loop(0, n_chunks, step, 0)


def _rmsnorm_rows(x, g):
    ms = jnp.mean(x * x, axis=-1, keepdims=True)
    return x * lax.rsqrt(ms + RMS_EPS) * g


def _layernorm_rows(x, g, b):
    mu = jnp.mean(x, axis=-1, keepdims=True)
    xc = x - mu
    var = jnp.mean(xc * xc, axis=-1, keepdims=True)
    return xc * lax.rsqrt(var + LN_EPS) * g + b


def _load_rows(x_ref, i, n):
    if len(x_ref.shape) == 2:
        r0 = i * n if isinstance(i, int) else pl.multiple_of(i * n, n)
        return x_ref[pl.ds(r0, n), :]
    steps = n // NSEQ
    return x_ref[pl.ds(i * steps, steps), :, :].reshape(n, x_ref.shape[2])


def _norm_in(x_ref, g_ref, h_scr, tile):
    rc = 32

    def body(i):
        r0 = pl.multiple_of(i * rc, rc)
        xr = _load_rows(x_ref, i, rc)
        h_scr[pl.ds(r0, rc), :] = _rmsnorm_rows(xr, g_ref[...]).astype(BF16)

    _row_loop(tile // rc, body)


def _out_proj(x_ref, m_scr, wout_ref, fg_ref, xo_ref, tile):
    out = jnp.dot(m_scr[...], wout_ref[...], preferred_element_type=F32)
    xn = _load_rows(x_ref, 0, tile) + out
    if fg_ref is not None:
        xn = _rmsnorm_rows(xn, fg_ref[...])
    xo_ref[...] = xn.reshape(xo_ref.shape)


def _start_history(ext_scr, state_ref, pad, hist, first_tile):
    if state_ref is None:
        @pl.when(first_tile)
        def _():
            ext_scr[0:pad * NSEQ, :] = jnp.zeros((pad * NSEQ, ext_scr.shape[1]), F32)
    else:
        ext_scr[(pad - hist) * NSEQ:pad * NSEQ, :] = state_ref[...]


def _carry_history(ext_scr, state_ref, pad, tile):
    if state_ref is None:
        ext_scr[0:pad * NSEQ, :] = ext_scr[tile:tile + pad * NSEQ, :]


def _pool_kernel(*refs, tile, has_state, has_final, offset):
    refs = list(refs)
    x_ref = refs.pop(0)
    state_ref = refs.pop(0) if has_state else None
    g_ref, win_ref, wgrp_ref, scale_ref, wout_ref = refs[:5]
    refs = refs[5:]
    fg_ref = refs.pop(0) if has_final else None
    xo_ref, st_ref, h_scr, ext_scr, z_scr, d_scr, m_scr = refs

    e = z_scr.shape[1]
    grp = e // len(POOL_WINDOWS)
    first_tile = pl.program_id(1) == 0
    base = POOL_PAD * NSEQ

    _norm_in(x_ref, g_ref, h_scr, tile)

    _start_history(ext_scr, state_ref, POOL_PAD, POOL_HIST, first_tile)
    h = h_scr[...]
    ext_scr[base:base + tile, :] = jnp.dot(h, win_ref[:, 0:e], preferred_element_type=F32)
    z_scr[...] = jnp.dot(h, win_ref[:, e:2 * e], preferred_element_type=F32)

    rc = 32

    def window_delta(r0, gi, inv_cnt):
        cols = slice(gi * grp, (gi + 1) * grp)
        w = POOL_WINDOWS[gi]
        cur = ext_scr[pl.ds(base + r0, rc), cols]
        acc = cur
        for k in range(1, w):
            acc = acc + ext_scr[pl.ds(base + r0 - k * NSEQ, rc), cols]
        mean = acc * (1.0 / w) if inv_cnt is None else acc * inv_cnt[gi]
        return (mean - cur).astype(BF16)

    def pool_body(i):
        r0 = pl.multiple_of(i * rc, rc)
        for gi in range(len(POOL_WINDOWS)):
            d_scr[pl.ds(r0, rc), gi * grp:(gi + 1) * grp] = window_delta(r0, gi, None)

    _row_loop(tile // rc, pool_body)

    if offset < POOL_HIST:
        @pl.when(first_tile)
        def _():
            for r0 in range(0, POOL_PAD * NSEQ, rc):
                row = r0 + lax.broadcasted_iota(jnp.int32, (rc, grp), 0)
                pos = offset + lax.shift_right_logical(row, NSEQ.bit_length() - 1)
                inv = [1.0 / jnp.minimum(pos + 1, w).astype(F32) for w in POOL_WINDOWS]
                for gi in range(len(POOL_WINDOWS)):
                    d_scr[r0:r0 + rc, gi * grp:(gi + 1) * grp] = window_delta(r0, gi, inv)

    for gi in range(len(POOL_WINDOWS)):
        cols = slice(gi * grp, (gi + 1) * grp)
        y = jnp.dot(d_scr[:, cols], wgrp_ref[gi], preferred_element_type=F32)
        m = y * scale_ref[:, cols] * _silu(z_scr[:, cols])
        m_scr[:, cols] = m.astype(BF16)

    st_ref[...] = ext_scr[base + tile - POOL_HIST * NSEQ:base + tile, :]
    _carry_history(ext_scr, state_ref, POOL_PAD, tile)
    _out_proj(x_ref, m_scr, wout_ref, fg_ref, xo_ref, tile)


def _conv_kernel(*refs, tile, has_state, has_final):
    refs = list(refs)
    x_ref = refs.pop(0)
    state_ref = refs.pop(0) if has_state else None
    g_ref, win_ref, cw_ref, cb_ref, ng_ref, nb_ref, wout_ref = refs[:7]
    refs = refs[7:]
    fg_ref = refs.pop(0) if has_final else None
    xo_ref, st_ref, h_scr, ext_scr, z_scr, y_scr, m_scr = refs

    e = z_scr.shape[1]
    first_tile = pl.program_id(1) == 0
    base = CONV_PAD * NSEQ

    _norm_in(x_ref, g_ref, h_scr, tile)

    _start_history(ext_scr, state_ref, CONV_PAD, CONV_HIST, first_tile)
    h = h_scr[...]
    cblk = 512
    for c in range(e // cblk):
        a = jnp.dot(h, win_ref[:, c * cblk:(c + 1) * cblk], preferred_element_type=F32)
        gl = jnp.dot(h, win_ref[:, e + c * cblk:e + (c + 1) * cblk], preferred_element_type=F32)
        ext_scr[base:base + tile, c * cblk:(c + 1) * cblk] = a * jax.nn.sigmoid(gl)
    z_scr[...] = jnp.dot(h, win_ref[:, 2 * e:3 * e], preferred_element_type=F32)

    rc = 32
    nq = rc // NSEQ
    first = (CONV_PAD - CONV_HIST) * NSEQ

    def conv_body(i):
        r0 = pl.multiple_of(i * rc, rc)
        for c in range(e // cblk):
            cols = slice(c * cblk, (c + 1) * cblk)
            acc = [cb_ref[:, cols] for _ in range(nq)]
            for k in range(CONV_WIDTH):
                wk = cw_ref[k, :, cols]
                for q in range(nq):
                    acc[q] = acc[q] + wk * ext_scr[pl.ds(first + r0 + (k + q) * NSEQ, NSEQ), cols]
            for q in range(nq):
                y_scr[pl.ds(r0 + q * NSEQ, NSEQ), cols] = acc[q]

    _row_loop(tile // rc, conv_body)

    rn = 16

    def norm_body(i):
        r0 = pl.multiple_of(i * rn, rn)
        yn = _layernorm_rows(y_scr[pl.ds(r0, rn), :], ng_ref[...], nb_ref[...])
        m = _silu(yn) * _silu(z_scr[pl.ds(r0, rn), :])
        m_scr[pl.ds(r0, rn), :] = m.astype(BF16)

    _row_loop(tile // rn, norm_body)

    st_ref[...] = ext_scr[base + tile - CONV_HIST * NSEQ:base + tile, :]
    _carry_history(ext_scr, state_ref, CONV_PAD, tile)
    _out_proj(x_ref, m_scr, wout_ref, fg_ref, xo_ref, tile)


def _sgu_kernel(*refs, tile, has_final):
    refs = list(refs)
    x_ref, g_ref, win_ref, ng_ref, nb_ref, ws_ref, bs_ref, wout_ref = refs[:8]
    refs = refs[8:]
    fg_ref = refs.pop(0) if has_final else None
    xo_ref, st_ref, h_scr, u_scr, v_scr, z_scr, vb_scr, m_scr = refs

    e = z_scr.shape[1]
    hd = e // SGU_HEADS
    chunk = ws_ref.shape[1]

    _norm_in(x_ref, g_ref, h_scr, tile)

    h = h_scr[...]
    u_scr[...] = jnp.dot(h, win_ref[:, 0:e], preferred_element_type=F32)
    v_scr[...] = jnp.dot(h, win_ref[:, e:2 * e], preferred_element_type=F32)
    z_scr[...] = jnp.dot(h, win_ref[:, 2 * e:3 * e], preferred_element_type=F32)

    rn = 16

    def norm_body(i):
        r0 = pl.multiple_of(i * rn, rn)
        vn = _layernorm_rows(v_scr[pl.ds(r0, rn), :], ng_ref[...], nb_ref[...])
        v_scr[pl.ds(r0, rn), :] = vn
        vb_scr[pl.ds(r0, rn), :] = vn.astype(BF16)

    _row_loop(tile // rn, norm_body)

    st_ref[...] = v_scr[tile - st_ref.shape[0]:tile, :]

    for c in range(tile // chunk):
        rows = slice(c * chunk, (c + 1) * chunk)
        for hh in range(SGU_HEADS):
            cols = slice(hh * hd, (hh + 1) * hd)
            sp = jnp.dot(ws_ref[hh], vb_scr[rows, cols], preferred_element_type=F32)
            sp = sp + pltpu.repeat(bs_ref[hh], hd // bs_ref.shape[2], axis=1)
            m = u_scr[rows, cols] * sp * _silu(z_scr[rows, cols])
            m_scr[rows, cols] = m.astype(BF16)

    _out_proj(x_ref, m_scr, wout_ref, fg_ref, xo_ref, tile)


def _resident(shape):
    nd = len(shape)
    return pl.BlockSpec(shape, lambda b, t: (0,) * nd, pipeline_mode=pl.Buffered(1))


def _params():
    return pltpu.CompilerParams(dimension_semantics=("arbitrary", "arbitrary"),
                                vmem_limit_bytes=VMEM_LIMIT_BYTES)


def _layer_call(kernel_fn, x, x_block, x_map, grid, state, weights, final_g, st_shape, scratch, name):
    operands = [x]
    x_spec = pl.BlockSpec(x_block, x_map)
    in_specs = [x_spec]
    if state is not None:
        operands.append(state)
        in_specs.append(pl.BlockSpec((None,) + state.shape[1:], lambda b, t: (b, 0, 0)))
    for w in weights:
        operands.append(w)
        in_specs.append(_resident(w.shape))
    if final_g is not None:
        operands.append(final_g)
        in_specs.append(_resident(final_g.shape))
    out_shape = (jax.ShapeDtypeStruct(x.shape, F32), jax.ShapeDtypeStruct(st_shape, F32))
    out_specs = (x_spec, pl.BlockSpec((None,) + st_shape[1:], lambda b, t: (b, 0, 0)))
    return pl.pallas_call(
        kernel_fn,
        grid=grid,
        in_specs=in_specs,
        out_specs=out_specs,
        out_shape=out_shape,
        scratch_shapes=scratch,
        compiler_params=_params(),
        name=name,
    )(*operands)


def _group_geometry(x):
    steps, nseq, d = x.shape
    tsteps = min(TILE_STEPS, steps)
    grid = (nseq // NSEQ, steps // tsteps)
    return tsteps * NSEQ, (tsteps, NSEQ, d), (lambda g, t: (t, g, 0)), grid


def _pool_layer(x, state, norm_g, in_w, grp_w, scale, out_w, final_g, *, offset, name):
    d = x.shape[2]
    e = out_w.shape[0]
    tile, x_block, x_map, grid = _group_geometry(x)
    kern = functools.partial(_pool_kernel, tile=tile, has_state=state is not None,
                             has_final=final_g is not None, offset=offset)
    scratch = [pltpu.VMEM((tile, d), BF16), pltpu.VMEM((POOL_PAD * NSEQ + tile, e), F32),
               pltpu.VMEM((tile, e), F32), pltpu.VMEM((tile, e), BF16), pltpu.VMEM((tile, e), BF16)]
    return _layer_call(kern, x, x_block, x_map, grid, state, [norm_g, in_w, grp_w, scale, out_w], final_g,
                       (grid[0], POOL_HIST * NSEQ, e), scratch, name)


def _conv_layer(x, state, norm_g, in_w, cw, cb, ng, nb, out_w, final_g, *, name):
    d = x.shape[2]
    e = out_w.shape[0]
    tile, x_block, x_map, grid = _group_geometry(x)
    kern = functools.partial(_conv_kernel, tile=tile, has_state=state is not None,
                             has_final=final_g is not None)
    scratch = [pltpu.VMEM((tile, d), BF16), pltpu.VMEM((CONV_PAD * NSEQ + tile, e), F32),
               pltpu.VMEM((tile, e), F32), pltpu.VMEM((tile, e), F32), pltpu.VMEM((tile, e), BF16)]
    return _layer_call(kern, x, x_block, x_map, grid, state, [norm_g, in_w, cw, cb, ng, nb, out_w], final_g,
                       (grid[0], CONV_HIST * NSEQ, e), scratch, name)


def _sgu_layer(x, norm_g, in_w, ng, nb, ws, bs, out_w, final_g, *, name):
    steps, nseq, d = x.shape
    e = out_w.shape[0]
    if steps >= SGU_CHUNK:
        tile = SGU_TILE
        x = x.reshape(steps, nseq * d)
        x_block, x_map, grid = (tile, d), (lambda b, t: (t, b)), (nseq, steps // tile)
        wm = jnp.where(jnp.tril(jnp.ones((SGU_CHUNK, SGU_CHUNK), dtype=bool))[None], ws, jnp.zeros_like(ws))
        bias = bs
        st_rows = SGU_CHUNK
    else:
        tile = steps * NSEQ
        x_block, x_map, grid = (steps, NSEQ, d), (lambda g, t: (t, g, 0)), (nseq // NSEQ, 1)
        wl = ws[:, :steps, :steps]
        wm = jnp.where(jnp.tril(jnp.ones((steps, steps), dtype=bool))[None], wl, jnp.zeros_like(wl))
        wm = jnp.einsum('hij,ab->hiajb', wm, jnp.eye(NSEQ, dtype=ws.dtype)).reshape(ws.shape[0], tile, tile)
        bias = jnp.repeat(bs[:, :steps], NSEQ, axis=1)
        st_rows = tile
    bias = jnp.broadcast_to(bias[:, :, None], bias.shape + (128,))
    kern = functools.partial(_sgu_kernel, tile=tile, has_final=final_g is not None)
    scratch = [pltpu.VMEM((tile, d), BF16), pltpu.VMEM((tile, e), F32), pltpu.VMEM((tile, e), F32),
               pltpu.VMEM((tile, e), F32), pltpu.VMEM((tile, e), BF16), pltpu.VMEM((tile, e), BF16)]
    xo, st = _layer_call(kern, x, x_block, x_map, grid, None,
                         [norm_g, in_w, ng, nb, wm.astype(BF16), bias, out_w], final_g,
                         (grid[0], st_rows, e), scratch, name)
    return xo.reshape(steps, nseq, d), st


def _to_step_major(state):
    n, s, e = state.shape
    return state.reshape(n // NSEQ, NSEQ, s, e).transpose(0, 2, 1, 3).reshape(n // NSEQ, s * NSEQ, e)


def _to_seq_major(state, steps):
    g, _, e = state.shape
    return state.reshape(g, steps, NSEQ, e).transpose(0, 2, 1, 3).reshape(g * NSEQ, steps, e)


def _run_group(x, state_pool, state_conv, offset, name, norm_g, pool_in_w, pool_w, pool_scale,
               pool_out_w, conv_in_w, conv_w, conv_b, conv_norm_g, conv_norm_b, conv_out_w,
               sgu_in_w, sgu_norm_g, sgu_norm_b, sgu_w, sgu_b, sgu_out_w, final_g):
    n_seq, seq_len, d = x.shape
    depth = norm_g.shape[0]
    xt = x.transpose(1, 0, 2)
    row = lambda v: v.reshape(1, -1)
    new_pool, new_conv, new_sgu = [], [], []
    for i in range(depth):
        kind, j = i % 3, i // 3
        fg = row(final_g) if i == depth - 1 else None
        lname = f"{name}_l{i}"
        if kind == 0:
            st = None if state_pool is None else _to_step_major(state_pool[j])
            xt, s = _pool_layer(xt, st, row(norm_g[i]), pool_in_w[j], pool_w[j], row(pool_scale[j]),
                                pool_out_w[j], fg, offset=offset, name=lname)
            new_pool.append(_to_seq_major(s, POOL_HIST))
        elif kind == 1:
            st = None if state_conv is None else _to_step_major(state_conv[j])
            cw = jnp.broadcast_to(conv_w[j][:, None, :], (CONV_WIDTH, NSEQ, conv_w.shape[2]))
            cb = jnp.broadcast_to(conv_b[j][None, :], (NSEQ, conv_b.shape[1]))
            xt, s = _conv_layer(xt, st, row(norm_g[i]), conv_in_w[j], cw, cb, row(conv_norm_g[j]),
                                row(conv_norm_b[j]), conv_out_w[j], fg, name=lname)
            new_conv.append(_to_seq_major(s, CONV_HIST))
        else:
            xt, s = _sgu_layer(xt, row(norm_g[i]), sgu_in_w[j], row(sgu_norm_g[j]), row(sgu_norm_b[j]),
                               sgu_w[j], sgu_b[j], sgu_out_w[j], fg, name=lname)
            new_sgu.append(s if seq_len >= SGU_CHUNK else _to_seq_major(s, seq_len))
    return xt.transpose(1, 0, 2), jnp.stack(new_pool), jnp.stack(new_conv), jnp.stack(new_sgu)


def kernel(x_prompt, x_sample, state_pool, state_conv, norm_g, pool_in_w, pool_w, pool_scale, pool_out_w,
           conv_in_w, conv_w, conv_b, conv_norm_g, conv_norm_b, conv_out_w,
           sgu_in_w, sgu_norm_g, sgu_norm_b, sgu_w, sgu_b, sgu_out_w, final_g):
    past_len = 4096
    weights = (norm_g, pool_in_w.astype(BF16), pool_w.astype(BF16), pool_scale, pool_out_w.astype(BF16),
               conv_in_w.astype(BF16), conv_w, conv_b, conv_norm_g, conv_norm_b, conv_out_w.astype(BF16),
               sgu_in_w.astype(BF16), sgu_norm_g, sgu_norm_b, sgu_w, sgu_b, sgu_out_w.astype(BF16), final_g)
    y_p, pool_p, conv_p, sgu_p = _run_group(x_prompt, None, None, 0, "prompt", *weights)
    y_s, pool_s, conv_s, sgu_s = _run_group(x_sample, state_pool, state_conv, past_len, "sample", *weights)
    return (y_p, y_s, pool_p, pool_s, conv_p, conv_s, sgu_p, sgu_s)
```

```python
import functools

import jax
import jax.numpy as jnp
from jax import lax
from jax.experimental import pallas as pl
from jax.experimental.pallas import tpu as pltpu

POOL_WINDOWS = (2, 4, 8, 16)
POOL_HIST = max(POOL_WINDOWS) - 1
CONV_WIDTH = 31
CONV_HIST = CONV_WIDTH - 1
SGU_CHUNK = 128
SGU_HEADS = 4
RMS_EPS = 1e-6
LN_EPS = 1e-5

NSEQ = 8
POOL_PAD = 16
CONV_PAD = 32
TILE_STEPS = 32
SGU_TILE = 256
VMEM_LIMIT_BYTES = 56 * 1024 * 1024

F32 = jnp.float32
BF16 = jnp.bfloat16


def _silu(v):
    return v * jax.nn.sigmoid(v)


def _row_loop(n_chunks, body):
    for i in range(n_chunks):
        body(i)


def _aligned(v, m):
    return v if isinstance(v, int) else pl.multiple_of(v, m)


def _rmsnorm_rows(x, g):
    ms = jnp.mean(x * x, axis=-1, keepdims=True)
    return x * lax.rsqrt(ms + RMS_EPS) * g


def _layernorm_rows(x, g, b):
    mu = jnp.mean(x, axis=-1, keepdims=True)
    xc = x - mu
    var = jnp.mean(xc * xc, axis=-1, keepdims=True)
    return xc * lax.rsqrt(var + LN_EPS) * g + b


def _load_rows(x_ref, i, n):
    if len(x_ref.shape) == 2:
        r0 = _aligned(i * n, n)
        return x_ref[pl.ds(r0, n), :]
    steps = n // NSEQ
    return x_ref[pl.ds(i * steps, steps), :, :].reshape(n, x_ref.shape[2])


def _norm_in(x_ref, g_ref, h_scr, tile):
    rc = 32

    def body(i):
        r0 = _aligned(i * rc, rc)
        xr = _load_rows(x_ref, i, rc)
        h_scr[pl.ds(r0, rc), :] = _rmsnorm_rows(xr, g_ref[...]).astype(BF16)

    _row_loop(tile // rc, body)


def _out_proj(x_ref, m_scr, wout_ref, fg_ref, xo_ref, tile):
    out = jnp.dot(m_scr[...], wout_ref[...], preferred_element_type=F32)
    xn = _load_rows(x_ref, 0, tile) + out
    if fg_ref is not None:
        xn = _rmsnorm_rows(xn, fg_ref[...])
    xo_ref[...] = xn.reshape(xo_ref.shape)


def _start_history(ext_scr, state_ref, pad, hist, first_tile):
    if state_ref is None:
        @pl.when(first_tile)
        def _():
            ext_scr[0:pad * NSEQ, :] = jnp.zeros((pad * NSEQ, ext_scr.shape[1]), F32)
    else:
        ext_scr[(pad - hist) * NSEQ:pad * NSEQ, :] = state_ref[...]


def _carry_history(ext_scr, state_ref, pad, tile):
    if state_ref is None:
        ext_scr[0:pad * NSEQ, :] = ext_scr[tile:tile + pad * NSEQ, :]


def _pool_kernel(*refs, tile, has_state, has_final, offset):
    refs = list(refs)
    x_ref = refs.pop(0)
    state_ref = refs.pop(0) if has_state else None
    g_ref, win_ref, wgrp_ref, scale_ref, wout_ref = refs[:5]
    refs = refs[5:]
    fg_ref = refs.pop(0) if has_final else None
    xo_ref, st_ref, h_scr, ext_scr, z_scr, d_scr, m_scr = refs

    e = z_scr.shape[1]
    grp = e // len(POOL_WINDOWS)
    first_tile = pl.program_id(1) == 0
    base = POOL_PAD * NSEQ

    _norm_in(x_ref, g_ref, h_scr, tile)

    _start_history(ext_scr, state_ref, POOL_PAD, POOL_HIST, first_tile)
    h = h_scr[...]
    ext_scr[base:base + tile, :] = jnp.dot(h, win_ref[:, 0:e], preferred_element_type=F32)
    z_scr[...] = jnp.dot(h, win_ref[:, e:2 * e], preferred_element_type=F32)

    rc = 32

    def window_delta(r0, gi, inv_cnt):
        cols = slice(gi * grp, (gi + 1) * grp)
        w = POOL_WINDOWS[gi]
        cur = ext_scr[pl.ds(base + r0, rc), cols]
        acc = cur
        for k in range(1, w):
            acc = acc + ext_scr[pl.ds(base + r0 - k * NSEQ, rc), cols]
        mean = acc * (1.0 / w) if inv_cnt is None else acc * inv_cnt[gi]
        return (mean - cur).astype(BF16)

    def pool_body(i):
        r0 = _aligned(i * rc, rc)
        for gi in range(len(POOL_WINDOWS)):
            d_scr[pl.ds(r0, rc), gi * grp:(gi + 1) * grp] = window_delta(r0, gi, None)

    _row_loop(tile // rc, pool_body)

    if offset < POOL_HIST:
        @pl.when(first_tile)
        def _():
            for r0 in range(0, POOL_PAD * NSEQ, rc):
                row = r0 + lax.broadcasted_iota(jnp.int32, (rc, grp), 0)
                pos = offset + lax.shift_right_logical(row, NSEQ.bit_length() - 1)
                inv = [1.0 / jnp.minimum(pos + 1, w).astype(F32) for w in POOL_WINDOWS]
                for gi in range(len(POOL_WINDOWS)):
                    d_scr[r0:r0 + rc, gi * grp:(gi + 1) * grp] = window_delta(r0, gi, inv)

    for gi in range(len(POOL_WINDOWS)):
        cols = slice(gi * grp, (gi + 1) * grp)
        y = jnp.dot(d_scr[:, cols], wgrp_ref[gi], preferred_element_type=F32)
        m = y * scale_ref[:, cols] * _silu(z_scr[:, cols])
        m_scr[:, cols] = m.astype(BF16)

    st_ref[...] = ext_scr[base + tile - POOL_HIST * NSEQ:base + tile, :]
    _carry_history(ext_scr, state_ref, POOL_PAD, tile)
    _out_proj(x_ref, m_scr, wout_ref, fg_ref, xo_ref, tile)


def _conv_kernel(*refs, tile, has_state, has_final):
    refs = list(refs)
    x_ref = refs.pop(0)
    state_ref = refs.pop(0) if has_state else None
    g_ref, win_ref, cw_ref, cb_ref, ng_ref, nb_ref, wout_ref = refs[:7]
    refs = refs[7:]
    fg_ref = refs.pop(0) if has_final else None
    xo_ref, st_ref, h_scr, ext_scr, z_scr, y_scr, m_scr = refs

    e = z_scr.shape[1]
    first_tile = pl.program_id(1) == 0
    base = CONV_PAD * NSEQ

    _norm_in(x_ref, g_ref, h_scr, tile)

    _start_history(ext_scr, state_ref, CONV_PAD, CONV_HIST, first_tile)
    h = h_scr[...]
    cblk = 512
    for c in range(e // cblk):
        a = jnp.dot(h, win_ref[:, c * cblk:(c + 1) * cblk], preferred_element_type=F32)
        gl = jnp.dot(h, win_ref[:, e + c * cblk:e + (c + 1) * cblk], preferred_element_type=F32)
        ext_scr[base:base + tile, c * cblk:(c + 1) * cblk] = a * jax.nn.sigmoid(gl)
    z_scr[...] = jnp.dot(h, win_ref[:, 2 * e:3 * e], preferred_element_type=F32)

    rc = 32
    nq = rc // NSEQ
    first = (CONV_PAD - CONV_HIST) * NSEQ

    def conv_body(i):
        r0 = _aligned(i * rc, rc)
        for c in range(e // cblk):
            cols = slice(c * cblk, (c + 1) * cblk)
            acc = [cb_ref[:, cols] for _ in range(nq)]
            for k in range(CONV_WIDTH):
                wk = cw_ref[k, :, cols]
                for q in range(nq):
                    acc[q] = acc[q] + wk * ext_scr[pl.ds(first + r0 + (k + q) * NSEQ, NSEQ), cols]
            for q in range(nq):
                y_scr[pl.ds(r0 + q * NSEQ, NSEQ), cols] = acc[q]

    _row_loop(tile // rc, conv_body)

    rn = 16

    def norm_body(i):
        r0 = _aligned(i * rn, rn)
        yn = _layernorm_rows(y_scr[pl.ds(r0, rn), :], ng_ref[...], nb_ref[...])
        m = _silu(yn) * _silu(z_scr[pl.ds(r0, rn), :])
        m_scr[pl.ds(r0, rn), :] = m.astype(BF16)

    _row_loop(tile // rn, norm_body)

    st_ref[...] = ext_scr[base + tile - CONV_HIST * NSEQ:base + tile, :]
    _carry_history(ext_scr, state_ref, CONV_PAD, tile)
    _out_proj(x_ref, m_scr, wout_ref, fg_ref, xo_ref, tile)


def _sgu_kernel(*refs, tile, has_final):
    refs = list(refs)
    x_ref, g_ref, win_ref, ng_ref, nb_ref, ws_ref, bs_ref, wout_ref = refs[:8]
    refs = refs[8:]
    fg_ref = refs.pop(0) if has_final else None
    xo_ref, st_ref, h_scr, u_scr, v_scr, z_scr, vb_scr, m_scr = refs

    e = z_scr.shape[1]
    hd = e // SGU_HEADS
    chunk = ws_ref.shape[1]

    _norm_in(x_ref, g_ref, h_scr, tile)

    h = h_scr[...]
    u_scr[...] = jnp.dot(h, win_ref[:, 0:e], preferred_element_type=F32)
    v_scr[...] = jnp.dot(h, win_ref[:, e:2 * e], preferred_element_type=F32)
    z_scr[...] = jnp.dot(h, win_ref[:, 2 * e:3 * e], preferred_element_type=F32)

    rn = 16

    def norm_body(i):
        r0 = _aligned(i * rn, rn)
        vn = _layernorm_rows(v_scr[pl.ds(r0, rn), :], ng_ref[...], nb_ref[...])
        v_scr[pl.ds(r0, rn), :] = vn
        vb_scr[pl.ds(r0, rn), :] = vn.astype(BF16)

    _row_loop(tile // rn, norm_body)

    st_ref[...] = v_scr[tile - st_ref.shape[0]:tile, :]

    for c in range(tile // chunk):
        rows = slice(c * chunk, (c + 1) * chunk)
        for hh in range(SGU_HEADS):
            cols = slice(hh * hd, (hh + 1) * hd)
            sp = jnp.dot(ws_ref[hh], vb_scr[rows, cols], preferred_element_type=F32)
            sp = sp + pltpu.repeat(bs_ref[hh], hd // bs_ref.shape[2], axis=1)
            m = u_scr[rows, cols] * sp * _silu(z_scr[rows, cols])
            m_scr[rows, cols] = m.astype(BF16)

    _out_proj(x_ref, m_scr, wout_ref, fg_ref, xo_ref, tile)


def _resident(shape):
    nd = len(shape)
    return pl.BlockSpec(shape, lambda b, t: (0,) * nd, pipeline_mode=pl.Buffered(1))


def _params():
    return pltpu.CompilerParams(dimension_semantics=("arbitrary", "arbitrary"),
                                vmem_limit_bytes=VMEM_LIMIT_BYTES)


def _layer_call(kernel_fn, x, x_block, x_map, grid, state, weights, final_g, st_shape, scratch, name):
    operands = [x]
    x_spec = pl.BlockSpec(x_block, x_map)
    in_specs = [x_spec]
    if state is not None:
        operands.append(state)
        in_specs.append(pl.BlockSpec((None,) + state.shape[1:], lambda b, t: (b, 0, 0)))
    for w in weights:
        operands.append(w)
        in_specs.append(_resident(w.shape))
    if final_g is not None:
        operands.append(final_g)
        in_specs.append(_resident(final_g.shape))
    out_shape = (jax.ShapeDtypeStruct(x.shape, F32), jax.ShapeDtypeStruct(st_shape, F32))
    out_specs = (x_spec, pl.BlockSpec((None,) + st_shape[1:], lambda b, t: (b, 0, 0)))
    return pl.pallas_call(
        kernel_fn,
        grid=grid,
        in_specs=in_specs,
        out_specs=out_specs,
        out_shape=out_shape,
        scratch_shapes=scratch,
        compiler_params=_params(),
        name=name,
    )(*operands)


def _group_geometry(x):
    steps, nseq, d = x.shape
    tsteps = min(TILE_STEPS, steps)
    grid = (nseq // NSEQ, steps // tsteps)
    return tsteps * NSEQ, (tsteps, NSEQ, d), (lambda g, t: (t, g, 0)), grid


def _pool_layer(x, state, norm_g, in_w, grp_w, scale, out_w, final_g, *, offset, name):
    d = x.shape[2]
    e = out_w.shape[0]
    tile, x_block, x_map, grid = _group_geometry(x)
    kern = functools.partial(_pool_kernel, tile=tile, has_state=state is not None,
                             has_final=final_g is not None, offset=offset)
    scratch = [pltpu.VMEM((tile, d), BF16), pltpu.VMEM((POOL_PAD * NSEQ + tile, e), F32),
               pltpu.VMEM((tile, e), F32), pltpu.VMEM((tile, e), BF16), pltpu.VMEM((tile, e), BF16)]
    return _layer_call(kern, x, x_block, x_map, grid, state, [norm_g, in_w, grp_w, scale, out_w], final_g,
                       (grid[0], POOL_HIST * NSEQ, e), scratch, name)


def _conv_layer(x, state, norm_g, in_w, cw, cb, ng, nb, out_w, final_g, *, name):
    d = x.shape[2]
    e = out_w.shape[0]
    tile, x_block, x_map, grid = _group_geometry(x)
    kern = functools.partial(_conv_kernel, tile=tile, has_state=state is not None,
                             has_final=final_g is not None)
    scratch = [pltpu.VMEM((tile, d), BF16), pltpu.VMEM((CONV_PAD * NSEQ + tile, e), F32),
               pltpu.VMEM((tile, e), F32), pltpu.VMEM((tile, e), F32), pltpu.VMEM((tile, e), BF16)]
    return _layer_call(kern, x, x_block, x_map, grid, state, [norm_g, in_w, cw, cb, ng, nb, out_w], final_g,
                       (grid[0], CONV_HIST * NSEQ, e), scratch, name)


def _sgu_layer(x, norm_g, in_w, ng, nb, ws, bs, out_w, final_g, *, name):
    steps, nseq, d = x.shape
    e = out_w.shape[0]
    if steps >= SGU_CHUNK:
        tile = SGU_TILE
        x = x.reshape(steps, nseq * d)
        x_block, x_map, grid = (tile, d), (lambda b, t: (t, b)), (nseq, steps // tile)
        wm = jnp.where(jnp.tril(jnp.ones((SGU_CHUNK, SGU_CHUNK), dtype=bool))[None], ws, jnp.zeros_like(ws))
        bias = bs
        st_rows = SGU_CHUNK
    else:
        tile = steps * NSEQ
        x_block, x_map, grid = (steps, NSEQ, d), (lambda g, t: (t, g, 0)), (nseq // NSEQ, 1)
        wl = ws[:, :steps, :steps]
        wm = jnp.where(jnp.tril(jnp.ones((steps, steps), dtype=bool))[None], wl, jnp.zeros_like(wl))
        wm = jnp.einsum('hij,ab->hiajb', wm, jnp.eye(NSEQ, dtype=ws.dtype)).reshape(ws.shape[0], tile, tile)
        bias = jnp.repeat(bs[:, :steps], NSEQ, axis=1)
        st_rows = tile
    bias = jnp.broadcast_to(bias[:, :, None], bias.shape + (128,))
    kern = functools.partial(_sgu_kernel, tile=tile, has_final=final_g is not None)
    scratch = [pltpu.VMEM((tile, d), BF16), pltpu.VMEM((tile, e), F32), pltpu.VMEM((tile, e), F32),
               pltpu.VMEM((tile, e), F32), pltpu.VMEM((tile, e), BF16), pltpu.VMEM((tile, e), BF16)]
    xo, st = _layer_call(kern, x, x_block, x_map, grid, None,
                         [norm_g, in_w, ng, nb, wm.astype(BF16), bias, out_w], final_g,
                         (grid[0], st_rows, e), scratch, name)
    return xo.reshape(steps, nseq, d), st


def _to_step_major(state):
    n, s, e = state.shape
    return state.reshape(n // NSEQ, NSEQ, s, e).transpose(0, 2, 1, 3).reshape(n // NSEQ, s * NSEQ, e)


def _to_seq_major(state, steps):
    g, _, e = state.shape
    return state.reshape(g, steps, NSEQ, e).transpose(0, 2, 1, 3).reshape(g * NSEQ, steps, e)


def _run_group(x, state_pool, state_conv, offset, name, norm_g, pool_in_w, pool_w, pool_scale,
               pool_out_w, conv_in_w, conv_w, conv_b, conv_norm_g, conv_norm_b, conv_out_w,
               sgu_in_w, sgu_norm_g, sgu_norm_b, sgu_w, sgu_b, sgu_out_w, final_g):
    n_seq, seq_len, d = x.shape
    depth = norm_g.shape[0]
    xt = x.transpose(1, 0, 2)
    row = lambda v: v.reshape(1, -1)
    new_pool, new_conv, new_sgu = [], [], []
    for i in range(depth):
        kind, j = i % 3, i // 3
        fg = row(final_g) if i == depth - 1 else None
        lname = f"{name}_l{i}"
        if kind == 0:
            st = None if state_pool is None else _to_step_major(state_pool[j])
            xt, s = _pool_layer(xt, st, row(norm_g[i]), pool_in_w[j], pool_w[j], row(pool_scale[j]),
                                pool_out_w[j], fg, offset=offset, name=lname)
            new_pool.append(_to_seq_major(s, POOL_HIST))
        elif kind == 1:
            st = None if state_conv is None else _to_step_major(state_conv[j])
            cw = jnp.broadcast_to(conv_w[j][:, None, :], (CONV_WIDTH, NSEQ, conv_w.shape[2]))
            cb = jnp.broadcast_to(conv_b[j][None, :], (NSEQ, conv_b.shape[1]))
            xt, s = _conv_layer(xt, st, row(norm_g[i]), conv_in_w[j], cw, cb, row(conv_norm_g[j]),
                                row(conv_norm_b[j]), conv_out_w[j], fg, name=lname)
            new_conv.append(_to_seq_major(s, CONV_HIST))
        else:
            xt, s = _sgu_layer(xt, row(norm_g[i]), sgu_in_w[j], row(sgu_norm_g[j]), row(sgu_norm_b[j]),
                               sgu_w[j], sgu_b[j], sgu_out_w[j], fg, name=lname)
            new_sgu.append(s if seq_len >= SGU_CHUNK else _to_seq_major(s, seq_len))
    return xt.transpose(1, 0, 2), jnp.stack(new_pool), jnp.stack(new_conv), jnp.stack(new_sgu)


def kernel(x_prompt, x_sample, state_pool, state_conv, norm_g, pool_in_w, pool_w, pool_scale, pool_out_w,
           conv_in_w, conv_w, conv_b, conv_norm_g, conv_norm_b, conv_out_w,
           sgu_in_w, sgu_norm_g, sgu_norm_b, sgu_w, sgu_b, sgu_out_w, final_g):
    past_len = 4096
    weights = (norm_g, pool_in_w.astype(BF16), pool_w.astype(BF16), pool_scale, pool_out_w.astype(BF16),
               conv_in_w.astype(BF16), conv_w, conv_b, conv_norm_g, conv_norm_b, conv_out_w.astype(BF16),
               sgu_in_w.astype(BF16), sgu_norm_g, sgu_norm_b, sgu_w, sgu_b, sgu_out_w.astype(BF16), final_g)
    y_p, pool_p, conv_p, sgu_p = _run_group(x_prompt, None, None, 0, "prompt", *weights)
    y_s, pool_s, conv_s, sgu_s = _run_group(x_sample, state_pool, state_conv, past_len, "sample", *weights)
    return (y_p, y_s, pool_p, pool_s, conv_p, conv_s, sgu_p, sgu_s)
```

```python
import functools

import jax
import jax.numpy as jnp
from jax import lax
from jax.experimental import pallas as pl
from jax.experimental.pallas import tpu as pltpu

POOL_WINDOWS = (2, 4, 8, 16)
POOL_HIST = max(POOL_WINDOWS) - 1
CONV_WIDTH = 31
CONV_HIST = CONV_WIDTH - 1
SGU_CHUNK = 128
SGU_HEADS = 4
RMS_EPS = 1e-6
LN_EPS = 1e-5

NSEQ = 8
LANES = 128
POOL_PAD = 16
CONV_PAD = 32
TILE_STEPS = 32
SGU_TILE = 256
PIPE_LAG = 2
CONV_BLOCKS = 8
VMEM_LIMIT_BYTES = 56 * 1024 * 1024

F32 = jnp.float32
BF16 = jnp.bfloat16


def _silu(v):
    return v * jax.nn.sigmoid(v)


def _rmsnorm_rows(x, g):
    ms = jnp.mean(x * x, axis=-1, keepdims=True)
    return x * lax.rsqrt(ms + RMS_EPS) * g


def _layernorm_rows(x, g, b):
    mu = jnp.mean(x, axis=-1, keepdims=True)
    xc = x - mu
    var = jnp.mean(xc * xc, axis=-1, keepdims=True)
    return xc * lax.rsqrt(var + LN_EPS) * g + b


def _load_rows(x_ref, r0, n):
    if len(x_ref.shape) == 2:
        return x_ref[r0:r0 + n, :]
    return x_ref[r0 // NSEQ:(r0 + n) // NSEQ, :, :].reshape(n, x_ref.shape[2])


def _norm_in(x_ref, g_ref, h_scr, tile):
    rc = 32
    for r0 in range(0, tile, rc):
        h_scr[r0:r0 + rc, :] = _rmsnorm_rows(_load_rows(x_ref, r0, rc), g_ref[...]).astype(BF16)


def _out_proj(xres_ref, m_scr, wout_ref, fg_ref, xo_ref, tile):
    out = jnp.dot(m_scr[...], wout_ref[...], preferred_element_type=F32)
    xn = _load_rows(xres_ref, 0, tile) + out
    if fg_ref is not None:
        xn = _rmsnorm_rows(xn, fg_ref[...])
    xo_ref[...] = xn.reshape(xo_ref.shape)


def _zero_at_start(*scratch):
    @pl.when(pl.program_id(0) == 0)
    def _():
        for ref in scratch:
            ref[...] = jnp.zeros(ref.shape, ref.dtype)


def _pool_kernel(*refs, tile, tps, has_state, has_final):
    refs = list(refs)
    x_ref, xres_ref = refs[:2]
    refs = refs[2:]
    state_ref = refs.pop(0) if has_state else None
    g_ref, win_ref, wgrp_ref, scale_ref, inv_ref, wout_ref = refs[:6]
    refs = refs[6:]
    fg_ref = refs.pop(0) if has_final else None
    xo_ref, st_ref, h_scr, ext_scr, z_scr, sg_scr, d_scr, m_scr = refs

    e = z_scr.shape[1]
    grp = e // len(POOL_WINDOWS)
    base = POOL_PAD * NSEQ
    hist = POOL_HIST * NSEQ
    rc = 32

    _zero_at_start(ext_scr, z_scr, m_scr)

    _out_proj(xres_ref, m_scr, wout_ref, fg_ref, xo_ref, tile)

    for r0 in range(0, tile, rc):
        sg_scr[r0:r0 + rc, :] = scale_ref[...] * _silu(z_scr[r0:r0 + rc, :])

    _norm_in(x_ref, g_ref, h_scr, tile)
    h = h_scr[...]

    prev = pl.program_id(0) - 1
    sel = jnp.where(prev % tps == 0, 0, 1)

    for gi, w in enumerate(POOL_WINDOWS):
        cols = slice(gi * grp, (gi + 1) * grp)
        for r0 in range(0, tile, rc):
            cur = ext_scr[base + r0:base + r0 + rc, cols]
            acc = cur
            for k in range(1, w):
                acc = acc + ext_scr[base + r0 - k * NSEQ:base + r0 - k * NSEQ + rc, cols]
            if r0 < base:
                inv = jnp.tile(inv_ref[sel, r0:r0 + rc, gi * LANES:(gi + 1) * LANES], (1, grp // LANES))
                mean = acc * inv
            else:
                mean = acc * (1.0 / w)
            d_scr[r0:r0 + rc, cols] = (mean - cur).astype(BF16)

        if state_ref is None:
            ext_scr[0:base, cols] = ext_scr[tile:tile + base, cols]
        else:
            ext_scr[base - hist:base, cols] = state_ref[:, cols]
        u = jnp.dot(h, win_ref[:, cols], preferred_element_type=F32)
        ext_scr[base:base + tile, cols] = u
        st_ref[:, cols] = u[tile - hist:, :]

        y = jnp.dot(d_scr[:, cols], wgrp_ref[gi], preferred_element_type=F32)
        m_scr[:, cols] = (y * sg_scr[:, cols]).astype(BF16)

    z_scr[...] = jnp.dot(h, win_ref[:, e:2 * e], preferred_element_type=F32)


def _conv_kernel(*refs, tile, tps, has_state, has_final):
    del tps
    refs = list(refs)
    x_ref, xres_ref = refs[:2]
    refs = refs[2:]
    state_ref = refs.pop(0) if has_state else None
    g_ref, win_ref, cw_ref, cb_ref, ng_ref, nb_ref, wout_ref = refs[:7]
    refs = refs[7:]
    fg_ref = refs.pop(0) if has_final else None
    xo_ref, st_ref, h_scr, ext_scr, a_scr, z_scr, sz_scr, y_scr, m_scr = refs

    nblk, _, cblk = z_scr.shape
    base = CONV_PAD * NSEQ
    hist = CONV_HIST * NSEQ
    first = base - hist
    rc = tile // nblk
    nq = rc // NSEQ

    _zero_at_start(ext_scr, z_scr, m_scr)

    _out_proj(xres_ref, m_scr, wout_ref, fg_ref, xo_ref, tile)

    for cb in range(nblk):
        for r0 in range(0, tile, 64):
            sz_scr[cb, r0:r0 + 64, :] = _silu(z_scr[cb, r0:r0 + 64, :])

    _norm_in(x_ref, g_ref, h_scr, tile)

    par = pl.program_id(0) % 2
    old = 1 - par
    for cb in range(nblk):
        if state_ref is None:
            ext_scr[par, cb, 0:base, :] = ext_scr[old, cb, tile:tile + base, :]
        else:
            ext_scr[par, cb, first:base, :] = state_ref[cb]

    def block_step(i, carry):
        r0 = pl.multiple_of(i * rc, rc)
        for cb in range(nblk):
            cols = slice(cb * cblk, (cb + 1) * cblk)
            acc = [cb_ref[:, cols] for _ in range(nq)]
            for k in range(CONV_WIDTH):
                wk = cw_ref[k, :, cols]
                for q in range(nq):
                    row = first + r0 + (k + q) * NSEQ
                    acc[q] = acc[q] + wk * ext_scr[old, cb, pl.ds(row, NSEQ), :]
            for q in range(nq):
                y_scr[pl.ds(r0 + q * NSEQ, NSEQ), cols] = acc[q]

        h = h_scr[...]
        a_scr[i] = jnp.dot(h, win_ref[0, i], preferred_element_type=F32)
        z_scr[i] = jnp.dot(h, win_ref[2, i], preferred_element_type=F32)
        return carry

    lax.fori_loop(0, nblk, block_step, 0)

    rn = tile // nblk
    h = h_scr[...]
    for cb in range(nblk):
        gl = jnp.dot(h, win_ref[1, cb], preferred_element_type=F32)
        glu = a_scr[cb] * jax.nn.sigmoid(gl)
        ext_scr[par, cb, base:base + tile, :] = glu
        st_ref[cb] = glu[tile - hist:, :]
        for r0 in range(cb * rn, (cb + 1) * rn, 16):
            yn = _layernorm_rows(y_scr[r0:r0 + 16, :], ng_ref[...], nb_ref[...])
            sz = jnp.concatenate([sz_scr[j, r0:r0 + 16, :] for j in range(nblk)], axis=1)
            m_scr[r0:r0 + 16, :] = (_silu(yn) * sz).astype(BF16)


def _sgu_kernel(*refs, tile, tps, has_final):
    del tps
    refs = list(refs)
    x_ref, xres_ref, g_ref, win_ref, ng_ref, nb_ref, ws_ref, bs_ref, wout_ref = refs[:9]
    refs = refs[9:]
    fg_ref = refs.pop(0) if has_final else None
    xo_ref, st_ref, h_scr, u_scr, v_scr, z_scr, uz_scr, vb_scr, m_scr = refs

    e = z_scr.shape[1]
    hd = e // SGU_HEADS
    chunk = ws_ref.shape[1]
    st_rows = st_ref.shape[0]
    rn = 16

    _zero_at_start(u_scr, v_scr, z_scr, m_scr)

    _out_proj(xres_ref, m_scr, wout_ref, fg_ref, xo_ref, tile)

    for r0 in range(0, tile, rn):
        uz_scr[r0:r0 + rn, :] = u_scr[r0:r0 + rn, :] * _silu(z_scr[r0:r0 + rn, :])
        vn = _layernorm_rows(v_scr[r0:r0 + rn, :], ng_ref[...], nb_ref[...])
        vb_scr[r0:r0 + rn, :] = vn.astype(BF16)
        if r0 >= tile - st_rows:
            st_ref[r0 - (tile - st_rows):r0 - (tile - st_rows) + rn, :] = vn

    _norm_in(x_ref, g_ref, h_scr, tile)
    h = h_scr[...]
    u_scr[...] = jnp.dot(h, win_ref[:, 0:e], preferred_element_type=F32)
    v_scr[...] = jnp.dot(h, win_ref[:, e:2 * e], preferred_element_type=F32)
    z_scr[...] = jnp.dot(h, win_ref[:, 2 * e:3 * e], preferred_element_type=F32)

    for c in range(tile // chunk):
        rows = slice(c * chunk, (c + 1) * chunk)
        for hh in range(SGU_HEADS):
            cols = slice(hh * hd, (hh + 1) * hd)
            sp = jnp.dot(ws_ref[hh], vb_scr[rows, cols], preferred_element_type=F32)
            sp = sp + jnp.tile(bs_ref[hh], (1, hd // bs_ref.shape[2]))
            m_scr[rows, cols] = (uz_scr[rows, cols] * sp).astype(BF16)


def _resident(shape):
    nd = len(shape)
    return pl.BlockSpec(shape, lambda s: (0,) * nd, pipeline_mode=pl.Buffered(1))


def _params():
    return pltpu.CompilerParams(dimension_semantics=("arbitrary",), vmem_limit_bytes=VMEM_LIMIT_BYTES)


def _layer_call(kernel_fn, x, x_block, tile_index, n_tiles, tps, state, weights, final_g, st_shape,
                st_stage, scratch, name):
    last = n_tiles - 1
    clamp = lambda t: jnp.clip(t, 0, last)
    operands = [x, x]
    in_specs = [pl.BlockSpec(x_block, lambda s: tile_index(clamp(s))),
                pl.BlockSpec(x_block, lambda s: tile_index(clamp(s - PIPE_LAG)))]
    if state is not None:
        operands.append(state)
        in_specs.append(pl.BlockSpec((None,) + state.shape[1:],
                                     lambda s: (clamp(s) // tps,) + (0,) * (state.ndim - 1)))
    for w in weights:
        operands.append(w)
        in_specs.append(_resident(w.shape))
    if final_g is not None:
        operands.append(final_g)
        in_specs.append(_resident(final_g.shape))
    out_shape = (jax.ShapeDtypeStruct(x.shape, F32), jax.ShapeDtypeStruct(st_shape, F32))
    out_specs = (pl.BlockSpec(x_block, lambda s: tile_index(clamp(s - PIPE_LAG))),
                 pl.BlockSpec((None,) + st_shape[1:],
                              lambda s: (clamp(s - st_stage) // tps,) + (0,) * (len(st_shape) - 1)))
    return pl.pallas_call(
        functools.partial(kernel_fn, tps=tps),
        grid=(n_tiles + PIPE_LAG,),
        in_specs=in_specs,
        out_specs=out_specs,
        out_shape=out_shape,
        scratch_shapes=scratch,
        compiler_params=_params(),
        name=name,
    )(*operands)


def _group_geometry(x, state):
    steps, nseq, d = x.shape
    tsteps = min(TILE_STEPS, steps)
    tps, groups = steps // tsteps, nseq // NSEQ
    assert state is not None or groups == 1
    assert state is None or tps == 1
    return tsteps * NSEQ, (tsteps, NSEQ, d), (lambda t: (t % tps, t // tps, 0)), tps * groups, tps


def _pool_counts(offset):
    pos = offset + jnp.arange(POOL_PAD * NSEQ) // NSEQ
    w = jnp.repeat(jnp.asarray(POOL_WINDOWS), LANES)
    start = 1.0 / jnp.minimum(pos[:, None] + 1, w[None, :]).astype(F32)
    steady = jnp.broadcast_to(1.0 / w.astype(F32), start.shape)
    return jnp.stack([start, steady])


def _pool_layer(x, state, norm_g, in_w, grp_w, scale, out_w, final_g, *, offset, name):
    d = x.shape[2]
    e = out_w.shape[0]
    tile, x_block, tile_index, n_tiles, tps = _group_geometry(x, state)
    kern = functools.partial(_pool_kernel, tile=tile, has_state=state is not None,
                             has_final=final_g is not None)
    scratch = [pltpu.VMEM((tile, d), BF16), pltpu.VMEM((POOL_PAD * NSEQ + tile, e), F32),
               pltpu.VMEM((tile, e), F32), pltpu.VMEM((tile, e), F32), pltpu.VMEM((tile, e), BF16),
               pltpu.VMEM((tile, e), BF16)]
    weights = [norm_g, in_w, grp_w, scale, _pool_counts(offset), out_w]
    return _layer_call(kern, x, x_block, tile_index, n_tiles, tps, state, weights, final_g,
                       (n_tiles // tps, POOL_HIST * NSEQ, e), 0, scratch, name)


def _conv_layer(x, state, norm_g, in_w, cw, cb, ng, nb, out_w, final_g, *, name):
    d = x.shape[2]
    e = out_w.shape[0]
    tile, x_block, tile_index, n_tiles, tps = _group_geometry(x, state)
    kern = functools.partial(_conv_kernel, tile=tile, has_state=state is not None,
                             has_final=final_g is not None)
    nblk, cblk = CONV_BLOCKS, e // CONV_BLOCKS
    hist = CONV_HIST * NSEQ
    scratch = [pltpu.VMEM((tile, d), BF16), pltpu.VMEM((2, nblk, CONV_PAD * NSEQ + tile, cblk), F32),
               pltpu.VMEM((nblk, tile, cblk), F32), pltpu.VMEM((nblk, tile, cblk), F32),
               pltpu.VMEM((nblk, tile, cblk), F32),
               pltpu.VMEM((tile, e), F32), pltpu.VMEM((tile, e), BF16)]
    in_w = in_w.reshape(d, 3, nblk, cblk).transpose(1, 2, 0, 3)
    if state is not None:
        state = state.reshape(-1, hist, nblk, cblk).transpose(0, 2, 1, 3)
    xo, st = _layer_call(kern, x, x_block, tile_index, n_tiles, tps, state,
                         [norm_g, in_w, cw, cb, ng, nb, out_w], final_g,
                         (n_tiles // tps, nblk, hist, cblk), 0, scratch, name)
    return xo, st.transpose(0, 2, 1, 3).reshape(-1, hist, e)


def _sgu_layer(x, norm_g, in_w, ng, nb, ws, bs, out_w, final_g, *, name):
    steps, nseq, d = x.shape
    e = out_w.shape[0]
    if steps >= SGU_CHUNK:
        tile = SGU_TILE
        tps, groups = steps // tile, nseq
        xk = x.reshape(steps, nseq * d)
        x_block, tile_index = (tile, d), (lambda t: (t % tps, t // tps))
        wm = jnp.where(jnp.tril(jnp.ones((SGU_CHUNK, SGU_CHUNK), dtype=bool))[None], ws, jnp.zeros_like(ws))
        bias = bs
        st_rows = SGU_CHUNK
    else:
        tile = steps * NSEQ
        tps, groups = 1, nseq // NSEQ
        xk = x
        x_block, tile_index = (steps, NSEQ, d), (lambda t: (0, t, 0))
        wl = ws[:, :steps, :steps]
        wm = jnp.where(jnp.tril(jnp.ones((steps, steps), dtype=bool))[None], wl, jnp.zeros_like(wl))
        wm = jnp.einsum('hij,ab->hiajb', wm, jnp.eye(NSEQ, dtype=ws.dtype)).reshape(ws.shape[0], tile, tile)
        bias = jnp.repeat(bs[:, :steps], NSEQ, axis=1)
        st_rows = tile
    bias = jnp.broadcast_to(bias[:, :, None], bias.shape + (LANES,))
    kern = functools.partial(_sgu_kernel, tile=tile, has_final=final_g is not None)
    scratch = [pltpu.VMEM((tile, d), BF16), pltpu.VMEM((tile, e), F32), pltpu.VMEM((tile, e), F32),
               pltpu.VMEM((tile, e), F32), pltpu.VMEM((tile, e), F32), pltpu.VMEM((tile, e), BF16),
               pltpu.VMEM((tile, e), BF16)]
    xo, st = _layer_call(kern, xk, x_block, tile_index, tps * groups, tps, None,
                         [norm_g, in_w, ng, nb, wm.astype(BF16), bias, out_w], final_g,
                         (groups, st_rows, e), 1, scratch, name)
    return xo.reshape(steps, nseq, d), st


def _to_step_major(state):
    n, s, e = state.shape
    return state.reshape(n // NSEQ, NSEQ, s, e).transpose(0, 2, 1, 3).reshape(n // NSEQ, s * NSEQ, e)


def _to_seq_major(state, steps):
    g, _, e = state.shape
    return state.reshape(g, steps, NSEQ, e).transpose(0, 2, 1, 3).reshape(g * NSEQ, steps, e)


def _run_group(x, state_pool, state_conv, offset, name, norm_g, pool_in_w, pool_w, pool_scale,
               pool_out_w, conv_in_w, conv_w, conv_b, conv_norm_g, conv_norm_b, conv_out_w,
               sgu_in_w, sgu_norm_g, sgu_norm_b, sgu_w, sgu_b, sgu_out_w, final_g):
    n_seq, seq_len, d = x.shape
    depth = norm_g.shape[0]
    xt = x.transpose(1, 0, 2)
    row = lambda v: v.reshape(1, -1)
    new_pool, new_conv, new_sgu = [], [], []
    for i in range(depth):
        kind, j = i % 3, i // 3
        fg = row(final_g) if i == depth - 1 else None
        lname = f"{name}_l{i}"
        if kind == 0:
            st = None if state_pool is None else _to_step_major(state_pool[j])
            xt, s = _pool_layer(xt, st, row(norm_g[i]), pool_in_w[j], pool_w[j], row(pool_scale[j]),
                                pool_out_w[j], fg, offset=offset, name=lname)
            new_pool.append(_to_seq_major(s, POOL_HIST))
        elif kind == 1:
            st = None if state_conv is None else _to_step_major(state_conv[j])
            cw = jnp.broadcast_to(conv_w[j][:, None, :], (CONV_WIDTH, NSEQ, conv_w.shape[2]))
            cb = jnp.broadcast_to(conv_b[j][None, :], (NSEQ, conv_b.shape[1]))
            xt, s = _conv_layer(xt, st, row(norm_g[i]), conv_in_w[j], cw, cb, row(conv_norm_g[j]),
                                row(conv_norm_b[j]), conv_out_w[j], fg, name=lname)
            new_conv.append(_to_seq_major(s, CONV_HIST))
        else:
            xt, s = _sgu_layer(xt, row(norm_g[i]), sgu_in_w[j], row(sgu_norm_g[j]), row(sgu_norm_b[j]),
                               sgu_w[j], sgu_b[j], sgu_out_w[j], fg, name=lname)
            new_sgu.append(s if seq_len >= SGU_CHUNK else _to_seq_major(s, seq_len))
    return xt.transpose(1, 0, 2), jnp.stack(new_pool), jnp.stack(new_conv), jnp.stack(new_sgu)


def kernel(x_prompt, x_sample, state_pool, state_conv, norm_g, pool_in_w, pool_w, pool_scale, pool_out_w,
           conv_in_w, conv_w, conv_b, conv_norm_g, conv_norm_b, conv_out_w,
           sgu_in_w, sgu_norm_g, sgu_norm_b, sgu_w, sgu_b, sgu_out_w, final_g):
    past_len = 4096
    weights = (norm_g, pool_in_w.astype(BF16), pool_w.astype(BF16), pool_scale, pool_out_w.astype(BF16),
               conv_in_w.astype(BF16), conv_w, conv_b, conv_norm_g, conv_norm_b, conv_out_w.astype(BF16),
               sgu_in_w.astype(BF16), sgu_norm_g, sgu_norm_b, sgu_w, sgu_b, sgu_out_w.astype(BF16), final_g)
    y_p, pool_p, conv_p, sgu_p = _run_group(x_prompt, None, None, 0, "prompt", *weights)
    y_s, pool_s, conv_s, sgu_s = _run_group(x_sample, state_pool, state_conv, past_len, "sample", *weights)
    return (y_p, y_s, pool_p, pool_s, conv_p, conv_s, sgu_p, sgu_s)
```

```python
import functools

import jax
import jax.numpy as jnp
from jax import lax
from jax.experimental import pallas as pl
from jax.experimental.pallas import tpu as pltpu

POOL_WINDOWS = (2, 4, 8, 16)
POOL_HIST = max(POOL_WINDOWS) - 1
CONV_WIDTH = 31
CONV_HIST = CONV_WIDTH - 1
SGU_CHUNK = 128
SGU_HEADS = 4
RMS_EPS = 1e-6
LN_EPS = 1e-5

NSEQ = 8
LANES = 128
POOL_PAD = 16
CONV_PAD = 32
POOL_TILE_STEPS = 64
CONV_TILE_STEPS = 32
SGU_TILE = 256
PIPE_LAG = 2
VMEM_LIMIT_BYTES = 56 * 1024 * 1024

F32 = jnp.float32
BF16 = jnp.bfloat16


def _silu(v):
    return v * jax.nn.sigmoid(v)


def _rmsnorm_rows(x, g):
    ms = jnp.mean(x * x, axis=-1, keepdims=True)
    return x * lax.rsqrt(ms + RMS_EPS) * g


def _layernorm_rows(x, g, b):
    mu = jnp.mean(x, axis=-1, keepdims=True)
    xc = x - mu
    var = jnp.mean(xc * xc, axis=-1, keepdims=True)
    return xc * lax.rsqrt(var + LN_EPS) * g + b


def _load_rows(x_ref, r0, n):
    if len(x_ref.shape) == 2:
        return x_ref[r0:r0 + n, :]
    return x_ref[r0 // NSEQ:(r0 + n) // NSEQ, :, :].reshape(n, x_ref.shape[2])


def _norm_in(x_ref, g_ref, h_scr, tile):
    rc = 32
    for r0 in range(0, tile, rc):
        h_scr[r0:r0 + rc, :] = _rmsnorm_rows(_load_rows(x_ref, r0, rc), g_ref[...]).astype(BF16)


def _out_proj(xres_ref, m_scr, wout_ref, fg_ref, xo_ref, tile):
    out = jnp.dot(m_scr[...], wout_ref[...], preferred_element_type=F32)
    xn = _load_rows(xres_ref, 0, tile) + out
    if fg_ref is not None:
        xn = _rmsnorm_rows(xn, fg_ref[...])
    xo_ref[...] = xn.reshape(xo_ref.shape)


def _zero_at_start(*scratch):
    @pl.when(pl.program_id(0) == 0)
    def _():
        for ref in scratch:
            ref[...] = jnp.zeros(ref.shape, ref.dtype)


def _pool_kernel(*refs, tile, tps, has_state, has_final):
    refs = list(refs)
    x_ref, xres_ref = refs[:2]
    refs = refs[2:]
    state_ref = refs.pop(0) if has_state else None
    g_ref, win_ref, wgrp_ref, scale_ref, inv_ref, wout_ref = refs[:6]
    refs = refs[6:]
    fg_ref = refs.pop(0) if has_final else None
    xo_ref, st_ref, h_scr, ext_scr, z_scr, sg_scr, d_scr, m_scr = refs

    e = z_scr.shape[1]
    grp = e // len(POOL_WINDOWS)
    base = POOL_PAD * NSEQ
    hist = POOL_HIST * NSEQ
    rc = 32

    _zero_at_start(ext_scr, z_scr, m_scr)

    _out_proj(xres_ref, m_scr, wout_ref, fg_ref, xo_ref, tile)

    for r0 in range(0, tile, rc):
        sg_scr[r0:r0 + rc, :] = scale_ref[...] * _silu(z_scr[r0:r0 + rc, :])

    _norm_in(x_ref, g_ref, h_scr, tile)
    h = h_scr[...]

    prev = pl.program_id(0) - 1
    sel = jnp.where(prev % tps == 0, 0, 1)

    for gi, w in enumerate(POOL_WINDOWS):
        cols = slice(gi * grp, (gi + 1) * grp)
        for r0 in range(0, tile, rc):
            cur = ext_scr[base + r0:base + r0 + rc, cols]
            acc = cur
            for k in range(1, w):
                acc = acc + ext_scr[base + r0 - k * NSEQ:base + r0 - k * NSEQ + rc, cols]
            if r0 < base:
                inv = jnp.tile(inv_ref[sel, r0:r0 + rc, gi * LANES:(gi + 1) * LANES], (1, grp // LANES))
                mean = acc * inv
            else:
                mean = acc * (1.0 / w)
            d_scr[r0:r0 + rc, cols] = (mean - cur).astype(BF16)

        if state_ref is None:
            ext_scr[0:base, cols] = ext_scr[tile:tile + base, cols]
        else:
            ext_scr[base - hist:base, cols] = state_ref[:, cols]
        u = jnp.dot(h, win_ref[:, cols], preferred_element_type=F32)
        ext_scr[base:base + tile, cols] = u
        st_ref[:, cols] = u[tile - hist:, :]

        y = jnp.dot(d_scr[:, cols], wgrp_ref[gi], preferred_element_type=F32)
        m_scr[:, cols] = (y * sg_scr[:, cols]).astype(BF16)

    z_scr[...] = jnp.dot(h, win_ref[:, e:2 * e], preferred_element_type=F32)


def _conv_kernel(*refs, tile, tps, has_state, has_final):
    refs = list(refs)
    x_ref, xres_ref = refs[:2]
    refs = refs[2:]
    state_ref = refs.pop(0) if has_state else None
    g_ref, win_ref, cw_ref, cb_ref, ng_ref, nb_ref, wout_ref = refs[:7]
    refs = refs[7:]
    fg_ref = refs.pop(0) if has_final else None
    xo_ref, st_ref, h_scr, ext_scr, z_scr, y_scr, m_scr = refs

    e = z_scr.shape[1]
    base = CONV_PAD * NSEQ
    hist = CONV_HIST * NSEQ
    first = base - hist
    cblk = 512
    rc = 32
    nq = rc // NSEQ

    _norm_in(x_ref, g_ref, h_scr, tile)

    if state_ref is None:
        @pl.when(pl.program_id(0) % tps == 0)
        def _():
            ext_scr[0:base, :] = jnp.zeros((base, e), F32)
    else:
        ext_scr[first:base, :] = state_ref[...]

    h = h_scr[...]
    for c in range(e // cblk):
        cols = slice(c * cblk, (c + 1) * cblk)
        a = jnp.dot(h, win_ref[:, c * cblk:(c + 1) * cblk], preferred_element_type=F32)
        gl = jnp.dot(h, win_ref[:, e + c * cblk:e + (c + 1) * cblk], preferred_element_type=F32)
        ext_scr[base:base + tile, cols] = a * jax.nn.sigmoid(gl)
    z_scr[...] = jnp.dot(h, win_ref[:, 2 * e:3 * e], preferred_element_type=F32)

    for r0 in range(0, tile, rc):
        for c in range(e // cblk):
            cols = slice(c * cblk, (c + 1) * cblk)
            acc = [cb_ref[:, cols] for _ in range(nq)]
            for k in range(CONV_WIDTH):
                wk = cw_ref[k, :, cols]
                for q in range(nq):
                    row = first + r0 + (k + q) * NSEQ
                    acc[q] = acc[q] + wk * ext_scr[row:row + NSEQ, cols]
            for q in range(nq):
                y_scr[r0 + q * NSEQ:r0 + (q + 1) * NSEQ, cols] = acc[q]

    for r0 in range(0, tile, 16):
        yn = _layernorm_rows(y_scr[r0:r0 + 16, :], ng_ref[...], nb_ref[...])
        m_scr[r0:r0 + 16, :] = (_silu(yn) * _silu(z_scr[r0:r0 + 16, :])).astype(BF16)

    st_ref[...] = ext_scr[base + tile - hist:base + tile, :]
    if state_ref is None:
        ext_scr[0:base, :] = ext_scr[tile:tile + base, :]
    _out_proj(xres_ref, m_scr, wout_ref, fg_ref, xo_ref, tile)


def _sgu_kernel(*refs, tile, tps, has_final):
    del tps
    refs = list(refs)
    x_ref, xres_ref, g_ref, win_ref, ng_ref, nb_ref, ws_ref, bs_ref, wout_ref = refs[:9]
    refs = refs[9:]
    fg_ref = refs.pop(0) if has_final else None
    xo_ref, st_ref, h_scr, u_scr, v_scr, z_scr, uz_scr, vb_scr, m_scr = refs

    e = z_scr.shape[1]
    hd = e // SGU_HEADS
    chunk = ws_ref.shape[1]
    st_rows = st_ref.shape[0]
    rn = 16

    _zero_at_start(u_scr, v_scr, z_scr, m_scr)

    _out_proj(xres_ref, m_scr, wout_ref, fg_ref, xo_ref, tile)

    for r0 in range(0, tile, rn):
        uz_scr[r0:r0 + rn, :] = u_scr[r0:r0 + rn, :] * _silu(z_scr[r0:r0 + rn, :])
        vn = _layernorm_rows(v_scr[r0:r0 + rn, :], ng_ref[...], nb_ref[...])
        vb_scr[r0:r0 + rn, :] = vn.astype(BF16)
        if r0 >= tile - st_rows:
            st_ref[r0 - (tile - st_rows):r0 - (tile - st_rows) + rn, :] = vn

    _norm_in(x_ref, g_ref, h_scr, tile)
    h = h_scr[...]
    u_scr[...] = jnp.dot(h, win_ref[:, 0:e], preferred_element_type=F32)
    v_scr[...] = jnp.dot(h, win_ref[:, e:2 * e], preferred_element_type=F32)
    z_scr[...] = jnp.dot(h, win_ref[:, 2 * e:3 * e], preferred_element_type=F32)

    for c in range(tile // chunk):
        rows = slice(c * chunk, (c + 1) * chunk)
        for hh in range(SGU_HEADS):
            cols = slice(hh * hd, (hh + 1) * hd)
            sp = jnp.dot(ws_ref[hh], vb_scr[rows, cols], preferred_element_type=F32)
            sp = sp + jnp.tile(bs_ref[hh], (1, hd // bs_ref.shape[2]))
            m_scr[rows, cols] = (uz_scr[rows, cols] * sp).astype(BF16)


def _resident(shape):
    nd = len(shape)
    return pl.BlockSpec(shape, lambda s: (0,) * nd, pipeline_mode=pl.Buffered(1))


def _params():
    return pltpu.CompilerParams(dimension_semantics=("arbitrary",), vmem_limit_bytes=VMEM_LIMIT_BYTES)


def _layer_call(kernel_fn, x, x_block, tile_index, n_tiles, tps, state, weights, final_g, st_shape,
                st_stage, scratch, name, lag=PIPE_LAG):
    last = n_tiles - 1
    clamp = lambda t: jnp.clip(t, 0, last)
    operands = [x, x]
    in_specs = [pl.BlockSpec(x_block, lambda s: tile_index(clamp(s))),
                pl.BlockSpec(x_block, lambda s: tile_index(clamp(s - lag)))]
    if state is not None:
        operands.append(state)
        in_specs.append(pl.BlockSpec((None,) + state.shape[1:],
                                     lambda s: (clamp(s) // tps,) + (0,) * (state.ndim - 1)))
    for w in weights:
        operands.append(w)
        in_specs.append(_resident(w.shape))
    if final_g is not None:
        operands.append(final_g)
        in_specs.append(_resident(final_g.shape))
    out_shape = (jax.ShapeDtypeStruct(x.shape, F32), jax.ShapeDtypeStruct(st_shape, F32))
    out_specs = (pl.BlockSpec(x_block, lambda s: tile_index(clamp(s - lag))),
                 pl.BlockSpec((None,) + st_shape[1:],
                              lambda s: (clamp(s - st_stage) // tps,) + (0,) * (len(st_shape) - 1)))
    return pl.pallas_call(
        functools.partial(kernel_fn, tps=tps),
        grid=(n_tiles + lag,),
        in_specs=in_specs,
        out_specs=out_specs,
        out_shape=out_shape,
        scratch_shapes=scratch,
        compiler_params=_params(),
        name=name,
    )(*operands)


def _group_geometry(x, state, tile_steps):
    steps, nseq, d = x.shape
    tsteps = min(tile_steps, steps)
    tps, groups = steps // tsteps, nseq // NSEQ
    assert state is not None or groups == 1
    assert state is None or tps == 1
    return tsteps * NSEQ, (tsteps, NSEQ, d), (lambda t: (t % tps, t // tps, 0)), tps * groups, tps


def _pool_counts(offset):
    pos = offset + jnp.arange(POOL_PAD * NSEQ) // NSEQ
    w = jnp.repeat(jnp.asarray(POOL_WINDOWS), LANES)
    start = 1.0 / jnp.minimum(pos[:, None] + 1, w[None, :]).astype(F32)
    steady = jnp.broadcast_to(1.0 / w.astype(F32), start.shape)
    return jnp.stack([start, steady])


def _pool_layer(x, state, norm_g, in_w, grp_w, scale, out_w, final_g, *, offset, name):
    d = x.shape[2]
    e = out_w.shape[0]
    tile, x_block, tile_index, n_tiles, tps = _group_geometry(x, state, POOL_TILE_STEPS)
    kern = functools.partial(_pool_kernel, tile=tile, has_state=state is not None,
                             has_final=final_g is not None)
    scratch = [pltpu.VMEM((tile, d), BF16), pltpu.VMEM((POOL_PAD * NSEQ + tile, e), F32),
               pltpu.VMEM((tile, e), F32), pltpu.VMEM((tile, e), F32), pltpu.VMEM((tile, e), BF16),
               pltpu.VMEM((tile, e), BF16)]
    weights = [norm_g, in_w, grp_w, scale, _pool_counts(offset), out_w]
    return _layer_call(kern, x, x_block, tile_index, n_tiles, tps, state, weights, final_g,
                       (n_tiles // tps, POOL_HIST * NSEQ, e), 0, scratch, name)


def _conv_layer(x, state, norm_g, in_w, cw, cb, ng, nb, out_w, final_g, *, name):
    d = x.shape[2]
    e = out_w.shape[0]
    tile, x_block, tile_index, n_tiles, tps = _group_geometry(x, state, CONV_TILE_STEPS)
    kern = functools.partial(_conv_kernel, tile=tile, has_state=state is not None,
                             has_final=final_g is not None)
    scratch = [pltpu.VMEM((tile, d), BF16), pltpu.VMEM((CONV_PAD * NSEQ + tile, e), F32),
               pltpu.VMEM((tile, e), F32), pltpu.VMEM((tile, e), F32), pltpu.VMEM((tile, e), BF16)]
    return _layer_call(kern, x, x_block, tile_index, n_tiles, tps, state,
                       [norm_g, in_w, cw, cb, ng, nb, out_w], final_g,
                       (n_tiles // tps, CONV_HIST * NSEQ, e), 0, scratch, name, lag=0)


def _sgu_layer(x, norm_g, in_w, ng, nb, ws, bs, out_w, final_g, *, name):
    steps, nseq, d = x.shape
    e = out_w.shape[0]
    if steps >= SGU_CHUNK:
        tile = SGU_TILE
        tps, groups = steps // tile, nseq
        xk = x.reshape(steps, nseq * d)
        x_block, tile_index = (tile, d), (lambda t: (t % tps, t // tps))
        wm = jnp.where(jnp.tril(jnp.ones((SGU_CHUNK, SGU_CHUNK), dtype=bool))[None], ws, jnp.zeros_like(ws))
        bias = bs
        st_rows = SGU_CHUNK
    else:
        tile = steps * NSEQ
        tps, groups = 1, nseq // NSEQ
        xk = x
        x_block, tile_index = (steps, NSEQ, d), (lambda t: (0, t, 0))
        wl = ws[:, :steps, :steps]
        wm = jnp.where(jnp.tril(jnp.ones((steps, steps), dtype=bool))[None], wl, jnp.zeros_like(wl))
        wm = jnp.einsum('hij,ab->hiajb', wm, jnp.eye(NSEQ, dtype=ws.dtype)).reshape(ws.shape[0], tile, tile)
        bias = jnp.repeat(bs[:, :steps], NSEQ, axis=1)
        st_rows = tile
    bias = jnp.broadcast_to(bias[:, :, None], bias.shape + (LANES,))
    kern = functools.partial(_sgu_kernel, tile=tile, has_final=final_g is not None)
    scratch = [pltpu.VMEM((tile, d), BF16), pltpu.VMEM((tile, e), F32), pltpu.VMEM((tile, e), F32),
               pltpu.VMEM((tile, e), F32), pltpu.VMEM((tile, e), F32), pltpu.VMEM((tile, e), BF16),
               pltpu.VMEM((tile, e), BF16)]
    xo, st = _layer_call(kern, xk, x_block, tile_index, tps * groups, tps, None,
                         [norm_g, in_w, ng, nb, wm.astype(BF16), bias, out_w], final_g,
                         (groups, st_rows, e), 1, scratch, name)
    return xo.reshape(steps, nseq, d), st


def _to_step_major(state):
    n, s, e = state.shape
    return state.reshape(n // NSEQ, NSEQ, s, e).transpose(0, 2, 1, 3).reshape(n // NSEQ, s * NSEQ, e)


def _to_seq_major(state, steps):
    g, _, e = state.shape
    return state.reshape(g, steps, NSEQ, e).transpose(0, 2, 1, 3).reshape(g * NSEQ, steps, e)


def _run_group(x, state_pool, state_conv, offset, name, norm_g, pool_in_w, pool_w, pool_scale,
               pool_out_w, conv_in_w, conv_w, conv_b, conv_norm_g, conv_norm_b, conv_out_w,
               sgu_in_w, sgu_norm_g, sgu_norm_b, sgu_w, sgu_b, sgu_out_w, final_g):
    n_seq, seq_len, d = x.shape
    depth = norm_g.shape[0]
    xt = x.transpose(1, 0, 2)
    row = lambda v: v.reshape(1, -1)
    new_pool, new_conv, new_sgu = [], [], []
    for i in range(depth):
        kind, j = i % 3, i // 3
        fg = row(final_g) if i == depth - 1 else None
        lname = f"{name}_l{i}"
        if kind == 0:
            st = None if state_pool is None else _to_step_major(state_pool[j])
            xt, s = _pool_layer(xt, st, row(norm_g[i]), pool_in_w[j], pool_w[j], row(pool_scale[j]),
                                pool_out_w[j], fg, offset=offset, name=lname)
            new_pool.append(_to_seq_major(s, POOL_HIST))
        elif kind == 1:
            st = None if state_conv is None else _to_step_major(state_conv[j])
            cw = jnp.broadcast_to(conv_w[j][:, None, :], (CONV_WIDTH, NSEQ, conv_w.shape[2]))
            cb = jnp.broadcast_to(conv_b[j][None, :], (NSEQ, conv_b.shape[1]))
            xt, s = _conv_layer(xt, st, row(norm_g[i]), conv_in_w[j], cw, cb, row(conv_norm_g[j]),
                                row(conv_norm_b[j]), conv_out_w[j], fg, name=lname)
            new_conv.append(_to_seq_major(s, CONV_HIST))
        else:
            xt, s = _sgu_layer(xt, row(norm_g[i]), sgu_in_w[j], row(sgu_norm_g[j]), row(sgu_norm_b[j]),
                               sgu_w[j], sgu_b[j], sgu_out_w[j], fg, name=lname)
            new_sgu.append(s if seq_len >= SGU_CHUNK else _to_seq_major(s, seq_len))
    return xt.transpose(1, 0, 2), jnp.stack(new_pool), jnp.stack(new_conv), jnp.stack(new_sgu)


def kernel(x_prompt, x_sample, state_pool, state_conv, norm_g, pool_in_w, pool_w, pool_scale, pool_out_w,
           conv_in_w, conv_w, conv_b, conv_norm_g, conv_norm_b, conv_out_w,
           sgu_in_w, sgu_norm_g, sgu_norm_b, sgu_w, sgu_b, sgu_out_w, final_g):
    past_len = 4096
    weights = (norm_g, pool_in_w.astype(BF16), pool_w.astype(BF16), pool_scale, pool_out_w.astype(BF16),
               conv_in_w.astype(BF16), conv_w, conv_b, conv_norm_g, conv_norm_b, conv_out_w.astype(BF16),
               sgu_in_w.astype(BF16), sgu_norm_g, sgu_norm_b, sgu_w, sgu_b, sgu_out_w.astype(BF16), final_g)
    y_p, pool_p, conv_p, sgu_p = _run_group(x_prompt, None, None, 0, "prompt", *weights)
    y_s, pool_s, conv_s, sgu_s = _run_group(x_sample, state_pool, state_conv, past_len, "sample", *weights)
    return (y_p, y_s, pool_p, pool_s, conv_p, conv_s, sgu_p, sgu_s)
```

```python
import functools

import jax
import jax.numpy as jnp
from jax import lax
from jax.experimental import pallas as pl
from jax.experimental.pallas import tpu as pltpu

POOL_WINDOWS = (2, 4, 8, 16)
POOL_HIST = max(POOL_WINDOWS) - 1
CONV_WIDTH = 31
CONV_HIST = CONV_WIDTH - 1
SGU_CHUNK = 128
SGU_HEADS = 4
RMS_EPS = 1e-6
LN_EPS = 1e-5

NSEQ = 8
LANES = 128
POOL_PAD = 16
CONV_PAD = 32
POOL_TILE_STEPS = 64
POOL_SEQ_TILE = 512
CONV_TILE_STEPS = 32
SGU_TILE = 256
PIPE_LAG = 2
VMEM_LIMIT_BYTES = 56 * 1024 * 1024

F32 = jnp.float32
BF16 = jnp.bfloat16


def _silu(v):
    return v * jax.nn.sigmoid(v)


def _rmsnorm_rows(x, g):
    ms = jnp.mean(x * x, axis=-1, keepdims=True)
    return x * lax.rsqrt(ms + RMS_EPS) * g


def _layernorm_rows(x, g, b):
    mu = jnp.mean(x, axis=-1, keepdims=True)
    xc = x - mu
    var = jnp.mean(xc * xc, axis=-1, keepdims=True)
    return xc * lax.rsqrt(var + LN_EPS) * g + b


def _load_rows(x_ref, r0, n):
    if len(x_ref.shape) == 2:
        return x_ref[r0:r0 + n, :]
    return x_ref[r0 // NSEQ:(r0 + n) // NSEQ, :, :].reshape(n, x_ref.shape[2])


def _norm_in(x_ref, g_ref, h_scr, tile):
    rc = 32
    for r0 in range(0, tile, rc):
        h_scr[r0:r0 + rc, :] = _rmsnorm_rows(_load_rows(x_ref, r0, rc), g_ref[...]).astype(BF16)


def _out_proj(xres_ref, m_scr, wout_ref, fg_ref, xo_ref, tile):
    out = jnp.dot(m_scr[...], wout_ref[...], preferred_element_type=F32)
    xn = _load_rows(xres_ref, 0, tile) + out
    if fg_ref is not None:
        xn = _rmsnorm_rows(xn, fg_ref[...])
    xo_ref[...] = xn.reshape(xo_ref.shape)


def _zero_at_start(*scratch):
    @pl.when(pl.program_id(0) == 0)
    def _():
        for ref in scratch:
            ref[...] = jnp.zeros(ref.shape, ref.dtype)


def _pool_kernel(*refs, tile, tps, n_tiles, stride, has_state, has_final):
    refs = list(refs)
    x_ref, xres_ref = refs[:2]
    refs = refs[2:]
    state_ref = refs.pop(0) if has_state else None
    g_ref, win_ref, wgrp_ref, scale_ref, inv_ref, wout_ref = refs[:6]
    refs = refs[6:]
    fg_ref = refs.pop(0) if has_final else None
    xo_ref, st_ref, h_scr, ext_scr, z_scr, sg_scr, d_scr, m_scr = refs

    e = z_scr.shape[1]
    grp = e // len(POOL_WINDOWS)
    base = POOL_PAD * stride
    hist = POOL_HIST * stride
    rc = 32 if stride == NSEQ else 64

    _zero_at_start(ext_scr, z_scr, m_scr)

    _out_proj(xres_ref, m_scr, wout_ref, fg_ref, xo_ref, tile)

    for r0 in range(0, tile, 32):
        sg_scr[r0:r0 + 32, :] = scale_ref[...] * _silu(z_scr[r0:r0 + 32, :])

    _norm_in(x_ref, g_ref, h_scr, tile)
    h = h_scr[...]

    prev = pl.program_id(0) - 1
    sel = jnp.where(prev % tps == 0, 0, 1)
    opens_group = jnp.minimum(pl.program_id(0), n_tiles - 1) % tps == 0

    for gi, w in enumerate(POOL_WINDOWS):
        cols = slice(gi * grp, (gi + 1) * grp)
        for r0 in range(0, tile, rc):
            if stride == NSEQ:
                cur = ext_scr[base + r0:base + r0 + rc, cols]
                acc = cur
                for k in range(1, w):
                    acc = acc + ext_scr[base + r0 - k * NSEQ:base + r0 - k * NSEQ + rc, cols]
            else:
                t = ext_scr[r0:r0 + base + rc, cols]
                cur = t[base:, :]
                sh = 1
                while sh < w:
                    t = t + pltpu.roll(t, sh, 0)
                    sh *= 2
                acc = t[base:, :]
            if r0 < base:
                nt = min(rc, base - r0)
                inv = jnp.tile(inv_ref[sel, r0:r0 + nt, gi * LANES:(gi + 1) * LANES], (1, grp // LANES))
                if nt < rc:
                    inv = jnp.concatenate([inv, jnp.full((rc - nt, grp), 1.0 / w, F32)], axis=0)
                mean = acc * inv
            else:
                mean = acc * (1.0 / w)
            d_scr[r0:r0 + rc, cols] = (mean - cur).astype(BF16)

        if state_ref is not None:
            ext_scr[base - hist:base, cols] = state_ref[:, cols]
        else:
            tail = ext_scr[tile:tile + base, cols]
            ext_scr[0:base, cols] = jnp.where(opens_group, jnp.zeros_like(tail), tail)
        u = jnp.dot(h, win_ref[:, cols], preferred_element_type=F32)
        ext_scr[base:base + tile, cols] = u
        st_ref[:, cols] = u[tile - hist:, :]

        y = jnp.dot(d_scr[:, cols], wgrp_ref[gi], preferred_element_type=F32)
        m_scr[:, cols] = (y * sg_scr[:, cols]).astype(BF16)

    z_scr[...] = jnp.dot(h, win_ref[:, e:2 * e], preferred_element_type=F32)


def _conv_kernel(*refs, tile, tps, has_state, has_final):
    refs = list(refs)
    x_ref, xres_ref = refs[:2]
    refs = refs[2:]
    state_ref = refs.pop(0) if has_state else None
    g_ref, win_ref, cw_ref, cb_ref, ng_ref, nb_ref, wout_ref = refs[:7]
    refs = refs[7:]
    fg_ref = refs.pop(0) if has_final else None
    xo_ref, st_ref, h_scr, ext_scr, z_scr, y_scr, m_scr = refs

    e = z_scr.shape[1]
    base = CONV_PAD * NSEQ
    hist = CONV_HIST * NSEQ
    first = base - hist
    cblk = 512
    rc = 32
    nq = rc // NSEQ

    _norm_in(x_ref, g_ref, h_scr, tile)

    if state_ref is None:
        @pl.when(pl.program_id(0) % tps == 0)
        def _():
            ext_scr[0:base, :] = jnp.zeros((base, e), F32)
    else:
        ext_scr[first:base, :] = state_ref[...]

    h = h_scr[...]
    for c in range(e // cblk):
        cols = slice(c * cblk, (c + 1) * cblk)
        a = jnp.dot(h, win_ref[:, c * cblk:(c + 1) * cblk], preferred_element_type=F32)
        gl = jnp.dot(h, win_ref[:, e + c * cblk:e + (c + 1) * cblk], preferred_element_type=F32)
        ext_scr[base:base + tile, cols] = a * jax.nn.sigmoid(gl)
    z_scr[...] = jnp.dot(h, win_ref[:, 2 * e:3 * e], preferred_element_type=F32)

    def conv_rows(i, carry):
        r0 = pl.multiple_of(i * rc, rc)
        for c in range(e // cblk):
            cols = slice(c * cblk, (c + 1) * cblk)
            acc = [cb_ref[:, cols] for _ in range(nq)]
            for k in range(CONV_WIDTH):
                wk = cw_ref[k, :, cols]
                for q in range(nq):
                    row = first + r0 + (k + q) * NSEQ
                    acc[q] = acc[q] + wk * ext_scr[pl.ds(row, NSEQ), cols]
            for q in range(nq):
                y_scr[pl.ds(r0 + q * NSEQ, NSEQ), cols] = acc[q]
        return carry

    lax.fori_loop(0, tile // rc, conv_rows, 0)

    for r0 in range(0, tile, 16):
        yn = _layernorm_rows(y_scr[r0:r0 + 16, :], ng_ref[...], nb_ref[...])
        m_scr[r0:r0 + 16, :] = (_silu(yn) * _silu(z_scr[r0:r0 + 16, :])).astype(BF16)

    st_ref[...] = ext_scr[base + tile - hist:base + tile, :]
    if state_ref is None:
        ext_scr[0:base, :] = ext_scr[tile:tile + base, :]
    _out_proj(xres_ref, m_scr, wout_ref, fg_ref, xo_ref, tile)


def _sgu_kernel(*refs, tile, tps, has_final):
    del tps
    refs = list(refs)
    x_ref, xres_ref, g_ref, win_ref, ng_ref, nb_ref, ws_ref, bs_ref, wout_ref = refs[:9]
    refs = refs[9:]
    fg_ref = refs.pop(0) if has_final else None
    xo_ref, st_ref, h_scr, u_scr, v_scr, z_scr, uz_scr, vb_scr, m_scr = refs

    e = z_scr.shape[1]
    hd = e // SGU_HEADS
    chunk = ws_ref.shape[1]
    st_rows = st_ref.shape[0]
    rn = 16

    _zero_at_start(u_scr, v_scr, z_scr, m_scr)

    _out_proj(xres_ref, m_scr, wout_ref, fg_ref, xo_ref, tile)

    for r0 in range(0, tile, rn):
        uz_scr[r0:r0 + rn, :] = u_scr[r0:r0 + rn, :] * _silu(z_scr[r0:r0 + rn, :])
        vn = _layernorm_rows(v_scr[r0:r0 + rn, :], ng_ref[...], nb_ref[...])
        vb_scr[r0:r0 + rn, :] = vn.astype(BF16)
        if r0 >= tile - st_rows:
            st_ref[r0 - (tile - st_rows):r0 - (tile - st_rows) + rn, :] = vn

    _norm_in(x_ref, g_ref, h_scr, tile)
    h = h_scr[...]
    u_scr[...] = jnp.dot(h, win_ref[:, 0:e], preferred_element_type=F32)
    v_scr[...] = jnp.dot(h, win_ref[:, e:2 * e], preferred_element_type=F32)
    z_scr[...] = jnp.dot(h, win_ref[:, 2 * e:3 * e], preferred_element_type=F32)

    for c in range(tile // chunk):
        rows = slice(c * chunk, (c + 1) * chunk)
        for hh in range(SGU_HEADS):
            cols = slice(hh * hd, (hh + 1) * hd)
            sp = jnp.dot(ws_ref[hh], vb_scr[rows, cols], preferred_element_type=F32)
            sp = sp + jnp.tile(bs_ref[hh], (1, hd // bs_ref.shape[2]))
            m_scr[rows, cols] = (uz_scr[rows, cols] * sp).astype(BF16)


def _resident(shape):
    nd = len(shape)
    return pl.BlockSpec(shape, lambda s: (0,) * nd, pipeline_mode=pl.Buffered(1))


def _params():
    return pltpu.CompilerParams(dimension_semantics=("arbitrary",), vmem_limit_bytes=VMEM_LIMIT_BYTES)


def _layer_call(kernel_fn, x, x_block, tile_index, n_tiles, tps, state, weights, final_g, st_shape,
                st_stage, scratch, name, lag=PIPE_LAG, out_x=None):
    out_shape_x, out_block, out_index = out_x or (x.shape, x_block, tile_index)
    last = n_tiles - 1
    clamp = lambda t: jnp.clip(t, 0, last)
    operands = [x, x]
    in_specs = [pl.BlockSpec(x_block, lambda s: tile_index(clamp(s))),
                pl.BlockSpec(x_block, lambda s: tile_index(clamp(s - lag)))]
    if state is not None:
        operands.append(state)
        in_specs.append(pl.BlockSpec((None,) + state.shape[1:],
                                     lambda s: (clamp(s) // tps,) + (0,) * (state.ndim - 1)))
    for w in weights:
        operands.append(w)
        in_specs.append(_resident(w.shape))
    if final_g is not None:
        operands.append(final_g)
        in_specs.append(_resident(final_g.shape))
    out_shape = (jax.ShapeDtypeStruct(out_shape_x, F32), jax.ShapeDtypeStruct(st_shape, F32))
    out_specs = (pl.BlockSpec(out_block, lambda s: out_index(clamp(s - lag))),
                 pl.BlockSpec((None,) + st_shape[1:],
                              lambda s: (clamp(s - st_stage) // tps,) + (0,) * (len(st_shape) - 1)))
    return pl.pallas_call(
        functools.partial(kernel_fn, tps=tps),
        grid=(n_tiles + lag,),
        in_specs=in_specs,
        out_specs=out_specs,
        out_shape=out_shape,
        scratch_shapes=scratch,
        compiler_params=_params(),
        name=name,
    )(*operands)


def _group_geometry(x, state, tile_steps):
    steps, nseq, d = x.shape
    tsteps = min(tile_steps, steps)
    tps, groups = steps // tsteps, nseq // NSEQ
    assert state is not None or groups == 1
    assert state is None or tps == 1
    return tsteps * NSEQ, (tsteps, NSEQ, d), (lambda t: (t % tps, t // tps, 0)), tps * groups, tps


def _seq_tiles(layout, steps, nseq, d, tile):
    tps = steps // tile
    if layout == "seq":
        return (nseq, steps, d), (None, tile, d), (lambda t: (t // tps, t % tps, 0))
    return (steps, nseq * d), (tile, d), (lambda t: (t % tps, t // tps))


def _pool_counts(offset, stride):
    pos = offset + jnp.arange(POOL_PAD * stride) // stride
    w = jnp.repeat(jnp.asarray(POOL_WINDOWS), LANES)
    start = 1.0 / jnp.minimum(pos[:, None] + 1, w[None, :]).astype(F32)
    steady = jnp.broadcast_to(1.0 / w.astype(F32), start.shape)
    return jnp.stack([start, steady])


def _pool_layer(x, state, norm_g, in_w, grp_w, scale, out_w, final_g, *, offset, name, layouts=None):
    e = out_w.shape[0]
    if layouts is None:
        d = x.shape[2]
        tile, x_block, tile_index, n_tiles, tps = _group_geometry(x, state, POOL_TILE_STEPS)
        stride, out_x = NSEQ, None
    else:
        nseq, steps, d = x.shape if layouts[0] == "seq" else (x.shape[1], x.shape[0], x.shape[2])
        tile = min(POOL_SEQ_TILE, steps)
        shape, x_block, tile_index = _seq_tiles(layouts[0], steps, nseq, d, tile)
        x = x.reshape(shape)
        out_x = _seq_tiles(layouts[1], steps, nseq, d, tile)
        tps, stride = steps // tile, 1
        n_tiles = tps * nseq
    kern = functools.partial(_pool_kernel, tile=tile, n_tiles=n_tiles, stride=stride,
                             has_state=state is not None, has_final=final_g is not None)
    scratch = [pltpu.VMEM((tile, d), BF16), pltpu.VMEM((POOL_PAD * stride + tile, e), F32),
               pltpu.VMEM((tile, e), F32), pltpu.VMEM((tile, e), F32), pltpu.VMEM((tile, e), BF16),
               pltpu.VMEM((tile, e), BF16)]
    weights = [norm_g, in_w, grp_w, scale, _pool_counts(offset, stride), out_w]
    xo, st = _layer_call(kern, x, x_block, tile_index, n_tiles, tps, state, weights, final_g,
                         (n_tiles // tps, POOL_HIST * stride, e), 0, scratch, name, out_x=out_x)
    if layouts is not None and layouts[1] == "time":
        xo = xo.reshape(steps, nseq, d)
    return xo, st


def _conv_layer(x, state, norm_g, in_w, cw, cb, ng, nb, out_w, final_g, *, name):
    d = x.shape[2]
    e = out_w.shape[0]
    tile, x_block, tile_index, n_tiles, tps = _group_geometry(x, state, CONV_TILE_STEPS)
    kern = functools.partial(_conv_kernel, tile=tile, has_state=state is not None,
                             has_final=final_g is not None)
    scratch = [pltpu.VMEM((tile, d), BF16), pltpu.VMEM((CONV_PAD * NSEQ + tile, e), F32),
               pltpu.VMEM((tile, e), F32), pltpu.VMEM((tile, e), F32), pltpu.VMEM((tile, e), BF16)]
    return _layer_call(kern, x, x_block, tile_index, n_tiles, tps, state,
                       [norm_g, in_w, cw, cb, ng, nb, out_w], final_g,
                       (n_tiles // tps, CONV_HIST * NSEQ, e), 0, scratch, name, lag=0)


def _sgu_layer(x, norm_g, in_w, ng, nb, ws, bs, out_w, final_g, *, name):
    steps, nseq, d = x.shape
    e = out_w.shape[0]
    if steps >= SGU_CHUNK:
        tile = SGU_TILE
        tps, groups = steps // tile, nseq
        xk = x.reshape(steps, nseq * d)
        x_block, tile_index = (tile, d), (lambda t: (t % tps, t // tps))
        wm = jnp.where(jnp.tril(jnp.ones((SGU_CHUNK, SGU_CHUNK), dtype=bool))[None], ws, jnp.zeros_like(ws))
        bias = bs
        st_rows = SGU_CHUNK
    else:
        tile = steps * NSEQ
        tps, groups = 1, nseq // NSEQ
        xk = x
        x_block, tile_index = (steps, NSEQ, d), (lambda t: (0, t, 0))
        wl = ws[:, :steps, :steps]
        wm = jnp.where(jnp.tril(jnp.ones((steps, steps), dtype=bool))[None], wl, jnp.zeros_like(wl))
        wm = jnp.einsum('hij,ab->hiajb', wm, jnp.eye(NSEQ, dtype=ws.dtype)).reshape(ws.shape[0], tile, tile)
        bias = jnp.repeat(bs[:, :steps], NSEQ, axis=1)
        st_rows = tile
    bias = jnp.broadcast_to(bias[:, :, None], bias.shape + (LANES,))
    kern = functools.partial(_sgu_kernel, tile=tile, has_final=final_g is not None)
    scratch = [pltpu.VMEM((tile, d), BF16), pltpu.VMEM((tile, e), F32), pltpu.VMEM((tile, e), F32),
               pltpu.VMEM((tile, e), F32), pltpu.VMEM((tile, e), F32), pltpu.VMEM((tile, e), BF16),
               pltpu.VMEM((tile, e), BF16)]
    xo, st = _layer_call(kern, xk, x_block, tile_index, tps * groups, tps, None,
                         [norm_g, in_w, ng, nb, wm.astype(BF16), bias, out_w], final_g,
                         (groups, st_rows, e), 1, scratch, name)
    return xo.reshape(steps, nseq, d), st


def _to_step_major(state):
    n, s, e = state.shape
    return state.reshape(n // NSEQ, NSEQ, s, e).transpose(0, 2, 1, 3).reshape(n // NSEQ, s * NSEQ, e)


def _to_seq_major(state, steps):
    g, _, e = state.shape
    return state.reshape(g, steps, NSEQ, e).transpose(0, 2, 1, 3).reshape(g * NSEQ, steps, e)


def _run_group(x, state_pool, state_conv, offset, name, norm_g, pool_in_w, pool_w, pool_scale,
               pool_out_w, conv_in_w, conv_w, conv_b, conv_norm_g, conv_norm_b, conv_out_w,
               sgu_in_w, sgu_norm_g, sgu_norm_b, sgu_w, sgu_b, sgu_out_w, final_g):
    n_seq, seq_len, d = x.shape
    depth = norm_g.shape[0]
    seq_pool = state_pool is None and seq_len >= POOL_SEQ_TILE
    seq_in = seq_pool and depth > 0
    seq_out = seq_pool and (depth - 1) % 3 == 0
    xt = x if seq_in else x.transpose(1, 0, 2)
    row = lambda v: v.reshape(1, -1)
    new_pool, new_conv, new_sgu = [], [], []
    for i in range(depth):
        kind, j = i % 3, i // 3
        fg = row(final_g) if i == depth - 1 else None
        lname = f"{name}_l{i}"
        if kind == 0 and seq_pool:
            layouts = ("seq" if i == 0 else "time", "seq" if i == depth - 1 else "time")
            xt, s = _pool_layer(xt, None, row(norm_g[i]), pool_in_w[j], pool_w[j], row(pool_scale[j]),
                                pool_out_w[j], fg, offset=offset, name=lname, layouts=layouts)
            new_pool.append(s)
        elif kind == 0:
            st = None if state_pool is None else _to_step_major(state_pool[j])
            xt, s = _pool_layer(xt, st, row(norm_g[i]), pool_in_w[j], pool_w[j], row(pool_scale[j]),
                                pool_out_w[j], fg, offset=offset, name=lname)
            new_pool.append(_to_seq_major(s, POOL_HIST))
        elif kind == 1:
            st = None if state_conv is None else _to_step_major(state_conv[j])
            cw = jnp.broadcast_to(conv_w[j][:, None, :], (CONV_WIDTH, NSEQ, conv_w.shape[2]))
            cb = jnp.broadcast_to(conv_b[j][None, :], (NSEQ, conv_b.shape[1]))
            xt, s = _conv_layer(xt, st, row(norm_g[i]), conv_in_w[j], cw, cb, row(conv_norm_g[j]),
                                row(conv_norm_b[j]), conv_out_w[j], fg, name=lname)
            new_conv.append(_to_seq_major(s, CONV_HIST))
        else:
            xt, s = _sgu_layer(xt, row(norm_g[i]), sgu_in_w[j], row(sgu_norm_g[j]), row(sgu_norm_b[j]),
                               sgu_w[j], sgu_b[j], sgu_out_w[j], fg, name=lname)
            new_sgu.append(s if seq_len >= SGU_CHUNK else _to_seq_major(s, seq_len))
    y = xt if seq_out else xt.transpose(1, 0, 2)
    return y, jnp.stack(new_pool), jnp.stack(new_conv), jnp.stack(new_sgu)


def kernel(x_prompt, x_sample, state_pool, state_conv, norm_g, pool_in_w, pool_w, pool_scale, pool_out_w,
           conv_in_w, conv_w, conv_b, conv_norm_g, conv_norm_b, conv_out_w,
           sgu_in_w, sgu_norm_g, sgu_norm_b, sgu_w, sgu_b, sgu_out_w, final_g):
    past_len = 4096
    weights = (norm_g, pool_in_w.astype(BF16), pool_w.astype(BF16), pool_scale, pool_out_w.astype(BF16),
               conv_in_w.astype(BF16), conv_w, conv_b, conv_norm_g, conv_norm_b, conv_out_w.astype(BF16),
               sgu_in_w.astype(BF16), sgu_norm_g, sgu_norm_b, sgu_w, sgu_b, sgu_out_w.astype(BF16), final_g)
    y_p, pool_p, conv_p, sgu_p = _run_group(x_prompt, None, None, 0, "prompt", *weights)
    y_s, pool_s, conv_s, sgu_s = _run_group(x_sample, state_pool, state_conv, past_len, "sample", *weights)
    return (y_p, y_s, pool_p, pool_s, conv_p, conv_s, sgu_p, sgu_s)
```

```python
import functools

import jax
import jax.numpy as jnp
from jax import lax
from jax.experimental import pallas as pl
from jax.experimental.pallas import tpu as pltpu

POOL_WINDOWS = (2, 4, 8, 16)
POOL_HIST = max(POOL_WINDOWS) - 1
CONV_WIDTH = 31
CONV_HIST = CONV_WIDTH - 1
SGU_CHUNK = 128
SGU_HEADS = 4
RMS_EPS = 1e-6
LN_EPS = 1e-5

NSEQ = 8
LANES = 128
POOL_PAD = 16
CONV_PAD = 32
POOL_TILE_STEPS = 64
POOL_SEQ_TILE = 512
CONV_TILE_STEPS = 32
SGU_TILE = 256
PIPE_LAG = 2
VMEM_LIMIT_BYTES = 56 * 1024 * 1024

F32 = jnp.float32
BF16 = jnp.bfloat16


def _silu(v):
    return v * jax.nn.sigmoid(v)


def _rmsnorm_rows(x, g):
    ms = jnp.mean(x * x, axis=-1, keepdims=True)
    return x * lax.rsqrt(ms + RMS_EPS) * g


def _layernorm_rows(x, g, b):
    mu = jnp.mean(x, axis=-1, keepdims=True)
    xc = x - mu
    var = jnp.mean(xc * xc, axis=-1, keepdims=True)
    return xc * lax.rsqrt(var + LN_EPS) * g + b


def _load_rows(x_ref, r0, n):
    if len(x_ref.shape) == 2:
        return x_ref[r0:r0 + n, :]
    return x_ref[r0 // NSEQ:(r0 + n) // NSEQ, :, :].reshape(n, x_ref.shape[2])


def _norm_in(x_ref, g_ref, h_scr, tile):
    rc = 32
    for r0 in range(0, tile, rc):
        h_scr[r0:r0 + rc, :] = _rmsnorm_rows(_load_rows(x_ref, r0, rc), g_ref[...]).astype(BF16)


def _out_proj(xres_ref, m_scr, wout_ref, fg_ref, xo_ref, tile):
    out = jnp.dot(m_scr[...], wout_ref[...], preferred_element_type=F32)
    xn = _load_rows(xres_ref, 0, tile) + out
    if fg_ref is not None:
        xn = _rmsnorm_rows(xn, fg_ref[...])
    xo_ref[...] = xn.reshape(xo_ref.shape)


def _zero_at_start(*scratch):
    @pl.when(pl.program_id(0) == 0)
    def _():
        for ref in scratch:
            ref[...] = jnp.zeros(ref.shape, ref.dtype)


def _pool_kernel(*refs, tile, tps, n_tiles, stride, has_state, has_final):
    refs = list(refs)
    x_ref, xres_ref = refs[:2]
    refs = refs[2:]
    state_ref = refs.pop(0) if has_state else None
    g_ref, win_ref, wgrp_ref, scale_ref, inv_ref, wout_ref = refs[:6]
    refs = refs[6:]
    fg_ref = refs.pop(0) if has_final else None
    xo_ref, st_ref, h_scr, ext_scr, z_scr, sg_scr, d_scr, m_scr = refs

    e = z_scr.shape[1]
    grp = e // len(POOL_WINDOWS)
    base = POOL_PAD * stride
    hist = POOL_HIST * stride
    rc = 32 if stride == NSEQ else 64

    _zero_at_start(ext_scr, z_scr, m_scr)

    _out_proj(xres_ref, m_scr, wout_ref, fg_ref, xo_ref, tile)

    for r0 in range(0, tile, 32):
        sg_scr[r0:r0 + 32, :] = scale_ref[...] * _silu(z_scr[r0:r0 + 32, :])

    _norm_in(x_ref, g_ref, h_scr, tile)
    h = h_scr[...]

    prev = pl.program_id(0) - 1
    sel = jnp.where(prev % tps == 0, 0, 1)
    opens_group = jnp.minimum(pl.program_id(0), n_tiles - 1) % tps == 0

    for gi, w in enumerate(POOL_WINDOWS):
        cols = slice(gi * grp, (gi + 1) * grp)
        for r0 in range(0, tile, rc):
            if stride == NSEQ:
                cur = ext_scr[base + r0:base + r0 + rc, cols]
                acc = cur
                for k in range(1, w):
                    acc = acc + ext_scr[base + r0 - k * NSEQ:base + r0 - k * NSEQ + rc, cols]
            else:
                t = ext_scr[r0:r0 + base + rc, cols]
                cur = t[base:, :]
                sh = 1
                while sh < w:
                    t = t + pltpu.roll(t, sh, 0)
                    sh *= 2
                acc = t[base:, :]
            if r0 < base:
                nt = min(rc, base - r0)
                inv = jnp.tile(inv_ref[sel, r0:r0 + nt, gi * LANES:(gi + 1) * LANES], (1, grp // LANES))
                if nt < rc:
                    inv = jnp.concatenate([inv, jnp.full((rc - nt, grp), 1.0 / w, F32)], axis=0)
                mean = acc * inv
            else:
                mean = acc * (1.0 / w)
            d_scr[r0:r0 + rc, cols] = (mean - cur).astype(BF16)

        if state_ref is not None:
            ext_scr[base - hist:base, cols] = state_ref[:, cols]
        else:
            tail = ext_scr[tile:tile + base, cols]
            ext_scr[0:base, cols] = jnp.where(opens_group, jnp.zeros_like(tail), tail)
        u = jnp.dot(h, win_ref[:, cols], preferred_element_type=F32)
        ext_scr[base:base + tile, cols] = u
        st_ref[:, cols] = u[tile - hist:, :]

        y = jnp.dot(d_scr[:, cols], wgrp_ref[gi], preferred_element_type=F32)
        m_scr[:, cols] = (y * sg_scr[:, cols]).astype(BF16)

    z_scr[...] = jnp.dot(h, win_ref[:, e:2 * e], preferred_element_type=F32)


def _conv_kernel(*refs, tile, tps, has_state, has_final):
    refs = list(refs)
    x_ref, xres_ref = refs[:2]
    refs = refs[2:]
    state_ref = refs.pop(0) if has_state else None
    g_ref, win_ref, cw_ref, cb_ref, ng_ref, nb_ref, wout_ref = refs[:7]
    refs = refs[7:]
    fg_ref = refs.pop(0) if has_final else None
    xo_ref, st_ref, h_scr, ext_scr, sz_scr, y_scr, m_scr = refs

    e = sz_scr.shape[1]
    base = CONV_PAD * NSEQ
    hist = CONV_HIST * NSEQ
    first = base - hist
    cblk = 512
    rc = 32
    nq = rc // NSEQ

    _zero_at_start(y_scr, sz_scr)

    if state_ref is None:
        @pl.when(pl.program_id(0) % tps == 0)
        def _():
            ext_scr[0:base, :] = jnp.zeros((base, e), F32)
    else:
        ext_scr[first:base, :] = state_ref[...]

    _norm_in(x_ref, g_ref, h_scr, tile)

    for r0 in range(0, tile, 16):
        yn = _layernorm_rows(y_scr[r0:r0 + 16, :], ng_ref[...], nb_ref[...])
        m_scr[r0:r0 + 16, :] = (_silu(yn) * sz_scr[r0:r0 + 16, :]).astype(BF16)

    h = h_scr[...]
    for c in range(e // cblk):
        cols = slice(c * cblk, (c + 1) * cblk)
        a = jnp.dot(h, win_ref[:, c * cblk:(c + 1) * cblk], preferred_element_type=F32)
        gl = jnp.dot(h, win_ref[:, e + c * cblk:e + (c + 1) * cblk], preferred_element_type=F32)
        ext_scr[base:base + tile, cols] = a * jax.nn.sigmoid(gl)
        sz_scr[:, cols] = _silu(jnp.dot(h, win_ref[:, 2 * e + c * cblk:2 * e + (c + 1) * cblk],
                                        preferred_element_type=F32))

    _out_proj(xres_ref, m_scr, wout_ref, fg_ref, xo_ref, tile)

    def conv_rows(i, carry):
        r0 = pl.multiple_of(i * rc, rc)
        for c in range(e // cblk):
            cols = slice(c * cblk, (c + 1) * cblk)
            acc = [cb_ref[:, cols] for _ in range(nq)]
            for k in range(CONV_WIDTH):
                wk = cw_ref[k, :, cols]
                for q in range(nq):
                    row = first + r0 + (k + q) * NSEQ
                    acc[q] = acc[q] + wk * ext_scr[pl.ds(row, NSEQ), cols]
            for q in range(nq):
                y_scr[pl.ds(r0 + q * NSEQ, NSEQ), cols] = acc[q]
        return carry

    lax.fori_loop(0, tile // rc, conv_rows, 0)

    st_ref[...] = ext_scr[base + tile - hist:base + tile, :]
    if state_ref is None:
        ext_scr[0:base, :] = ext_scr[tile:tile + base, :]


def _sgu_kernel(*refs, tile, tps, has_final):
    del tps
    refs = list(refs)
    x_ref, xres_ref, g_ref, win_ref, ng_ref, nb_ref, ws_ref, bs_ref, wout_ref = refs[:9]
    refs = refs[9:]
    fg_ref = refs.pop(0) if has_final else None
    xo_ref, st_ref, h_scr, u_scr, v_scr, z_scr, uz_scr, vb_scr, m_scr = refs

    e = z_scr.shape[1]
    hd = e // SGU_HEADS
    chunk = ws_ref.shape[1]
    st_rows = st_ref.shape[0]
    rn = 16

    _zero_at_start(u_scr, v_scr, z_scr, m_scr)

    _out_proj(xres_ref, m_scr, wout_ref, fg_ref, xo_ref, tile)

    for r0 in range(0, tile, rn):
        uz_scr[r0:r0 + rn, :] = u_scr[r0:r0 + rn, :] * _silu(z_scr[r0:r0 + rn, :])
        vn = _layernorm_rows(v_scr[r0:r0 + rn, :], ng_ref[...], nb_ref[...])
        vb_scr[r0:r0 + rn, :] = vn.astype(BF16)
        if r0 >= tile - st_rows:
            st_ref[r0 - (tile - st_rows):r0 - (tile - st_rows) + rn, :] = vn

    _norm_in(x_ref, g_ref, h_scr, tile)
    h = h_scr[...]
    u_scr[...] = jnp.dot(h, win_ref[:, 0:e], preferred_element_type=F32)
    v_scr[...] = jnp.dot(h, win_ref[:, e:2 * e], preferred_element_type=F32)
    z_scr[...] = jnp.dot(h, win_ref[:, 2 * e:3 * e], preferred_element_type=F32)

    for c in range(tile // chunk):
        rows = slice(c * chunk, (c + 1) * chunk)
        for hh in range(SGU_HEADS):
            cols = slice(hh * hd, (hh + 1) * hd)
            sp = jnp.dot(ws_ref[hh], vb_scr[rows, cols], preferred_element_type=F32)
            sp = sp + jnp.tile(bs_ref[hh], (1, hd // bs_ref.shape[2]))
            m_scr[rows, cols] = (uz_scr[rows, cols] * sp).astype(BF16)


def _resident(shape):
    nd = len(shape)
    return pl.BlockSpec(shape, lambda s: (0,) * nd, pipeline_mode=pl.Buffered(1))


def _params():
    return pltpu.CompilerParams(dimension_semantics=("arbitrary",), vmem_limit_bytes=VMEM_LIMIT_BYTES)


def _layer_call(kernel_fn, x, x_block, tile_index, n_tiles, tps, state, weights, final_g, st_shape,
                st_stage, scratch, name, lag=PIPE_LAG, out_x=None):
    out_shape_x, out_block, out_index = out_x or (x.shape, x_block, tile_index)
    last = n_tiles - 1
    clamp = lambda t: jnp.clip(t, 0, last)
    operands = [x, x]
    in_specs = [pl.BlockSpec(x_block, lambda s: tile_index(clamp(s))),
                pl.BlockSpec(x_block, lambda s: tile_index(clamp(s - lag)))]
    if state is not None:
        operands.append(state)
        in_specs.append(pl.BlockSpec((None,) + state.shape[1:],
                                     lambda s: (clamp(s) // tps,) + (0,) * (state.ndim - 1)))
    for w in weights:
        operands.append(w)
        in_specs.append(_resident(w.shape))
    if final_g is not None:
        operands.append(final_g)
        in_specs.append(_resident(final_g.shape))
    out_shape = (jax.ShapeDtypeStruct(out_shape_x, F32), jax.ShapeDtypeStruct(st_shape, F32))
    out_specs = (pl.BlockSpec(out_block, lambda s: out_index(clamp(s - lag))),
                 pl.BlockSpec((None,) + st_shape[1:],
                              lambda s: (clamp(s - st_stage) // tps,) + (0,) * (len(st_shape) - 1)))
    return pl.pallas_call(
        functools.partial(kernel_fn, tps=tps),
        grid=(n_tiles + lag,),
        in_specs=in_specs,
        out_specs=out_specs,
        out_shape=out_shape,
        scratch_shapes=scratch,
        compiler_params=_params(),
        name=name,
    )(*operands)


def _group_geometry(x, state, tile_steps):
    steps, nseq, d = x.shape
    tsteps = min(tile_steps, steps)
    tps, groups = steps // tsteps, nseq // NSEQ
    assert state is not None or groups == 1
    assert state is None or tps == 1
    return tsteps * NSEQ, (tsteps, NSEQ, d), (lambda t: (t % tps, t // tps, 0)), tps * groups, tps


def _seq_tiles(layout, steps, nseq, d, tile):
    tps = steps // tile
    if layout == "seq":
        return (nseq, steps, d), (None, tile, d), (lambda t: (t // tps, t % tps, 0))
    return (steps, nseq * d), (tile, d), (lambda t: (t % tps, t // tps))


def _pool_counts(offset, stride):
    pos = offset + jnp.arange(POOL_PAD * stride) // stride
    w = jnp.repeat(jnp.asarray(POOL_WINDOWS), LANES)
    start = 1.0 / jnp.minimum(pos[:, None] + 1, w[None, :]).astype(F32)
    steady = jnp.broadcast_to(1.0 / w.astype(F32), start.shape)
    return jnp.stack([start, steady])


def _pool_layer(x, state, norm_g, in_w, grp_w, scale, out_w, final_g, *, offset, name, layouts=None):
    e = out_w.shape[0]
    if layouts is None:
        d = x.shape[2]
        tile, x_block, tile_index, n_tiles, tps = _group_geometry(x, state, POOL_TILE_STEPS)
        stride, out_x = NSEQ, None
    else:
        nseq, steps, d = x.shape if layouts[0] == "seq" else (x.shape[1], x.shape[0], x.shape[2])
        tile = min(POOL_SEQ_TILE, steps)
        shape, x_block, tile_index = _seq_tiles(layouts[0], steps, nseq, d, tile)
        x = x.reshape(shape)
        out_x = _seq_tiles(layouts[1], steps, nseq, d, tile)
        tps, stride = steps // tile, 1
        n_tiles = tps * nseq
    kern = functools.partial(_pool_kernel, tile=tile, n_tiles=n_tiles, stride=stride,
                             has_state=state is not None, has_final=final_g is not None)
    scratch = [pltpu.VMEM((tile, d), BF16), pltpu.VMEM((POOL_PAD * stride + tile, e), F32),
               pltpu.VMEM((tile, e), F32), pltpu.VMEM((tile, e), F32), pltpu.VMEM((tile, e), BF16),
               pltpu.VMEM((tile, e), BF16)]
    weights = [norm_g, in_w, grp_w, scale, _pool_counts(offset, stride), out_w]
    xo, st = _layer_call(kern, x, x_block, tile_index, n_tiles, tps, state, weights, final_g,
                         (n_tiles // tps, POOL_HIST * stride, e), 0, scratch, name, out_x=out_x)
    if layouts is not None and layouts[1] == "time":
        xo = xo.reshape(steps, nseq, d)
    return xo, st


def _conv_layer(x, state, norm_g, in_w, cw, cb, ng, nb, out_w, final_g, *, name):
    d = x.shape[2]
    e = out_w.shape[0]
    tile, x_block, tile_index, n_tiles, tps = _group_geometry(x, state, CONV_TILE_STEPS)
    kern = functools.partial(_conv_kernel, tile=tile, has_state=state is not None,
                             has_final=final_g is not None)
    scratch = [pltpu.VMEM((tile, d), BF16), pltpu.VMEM((CONV_PAD * NSEQ + tile, e), F32),
               pltpu.VMEM((tile, e), F32), pltpu.VMEM((tile, e), F32), pltpu.VMEM((tile, e), BF16)]
    return _layer_call(kern, x, x_block, tile_index, n_tiles, tps, state,
                       [norm_g, in_w, cw, cb, ng, nb, out_w], final_g,
                       (n_tiles // tps, CONV_HIST * NSEQ, e), 0, scratch, name, lag=1)


def _sgu_layer(x, norm_g, in_w, ng, nb, ws, bs, out_w, final_g, *, name):
    steps, nseq, d = x.shape
    e = out_w.shape[0]
    if steps >= SGU_CHUNK:
        tile = SGU_TILE
        tps, groups = steps // tile, nseq
        xk = x.reshape(steps, nseq * d)
        x_block, tile_index = (tile, d), (lambda t: (t % tps, t // tps))
        wm = jnp.where(jnp.tril(jnp.ones((SGU_CHUNK, SGU_CHUNK), dtype=bool))[None], ws, jnp.zeros_like(ws))
        bias = bs
        st_rows = SGU_CHUNK
    else:
        tile = steps * NSEQ
        tps, groups = 1, nseq // NSEQ
        xk = x
        x_block, tile_index = (steps, NSEQ, d), (lambda t: (0, t, 0))
        wl = ws[:, :steps, :steps]
        wm = jnp.where(jnp.tril(jnp.ones((steps, steps), dtype=bool))[None], wl, jnp.zeros_like(wl))
        wm = jnp.einsum('hij,ab->hiajb', wm, jnp.eye(NSEQ, dtype=ws.dtype)).reshape(ws.shape[0], tile, tile)
        bias = jnp.repeat(bs[:, :steps], NSEQ, axis=1)
        st_rows = tile
    bias = jnp.broadcast_to(bias[:, :, None], bias.shape + (LANES,))
    kern = functools.partial(_sgu_kernel, tile=tile, has_final=final_g is not None)
    scratch = [pltpu.VMEM((tile, d), BF16), pltpu.VMEM((tile, e), F32), pltpu.VMEM((tile, e), F32),
               pltpu.VMEM((tile, e), F32), pltpu.VMEM((tile, e), F32), pltpu.VMEM((tile, e), BF16),
               pltpu.VMEM((tile, e), BF16)]
    xo, st = _layer_call(kern, xk, x_block, tile_index, tps * groups, tps, None,
                         [norm_g, in_w, ng, nb, wm.astype(BF16), bias, out_w], final_g,
                         (groups, st_rows, e), 1, scratch, name)
    return xo.reshape(steps, nseq, d), st


def _to_step_major(state):
    n, s, e = state.shape
    return state.reshape(n // NSEQ, NSEQ, s, e).transpose(0, 2, 1, 3).reshape(n // NSEQ, s * NSEQ, e)


def _to_seq_major(state, steps):
    g, _, e = state.shape
    return state.reshape(g, steps, NSEQ, e).transpose(0, 2, 1, 3).reshape(g * NSEQ, steps, e)


def _run_group(x, state_pool, state_conv, offset, name, norm_g, pool_in_w, pool_w, pool_scale,
               pool_out_w, conv_in_w, conv_w, conv_b, conv_norm_g, conv_norm_b, conv_out_w,
               sgu_in_w, sgu_norm_g, sgu_norm_b, sgu_w, sgu_b, sgu_out_w, final_g):
    n_seq, seq_len, d = x.shape
    depth = norm_g.shape[0]
    seq_pool = state_pool is None and seq_len >= POOL_SEQ_TILE
    seq_in = seq_pool and depth > 0
    seq_out = seq_pool and (depth - 1) % 3 == 0
    xt = x if seq_in else x.transpose(1, 0, 2)
    row = lambda v: v.reshape(1, -1)
    new_pool, new_conv, new_sgu = [], [], []
    for i in range(depth):
        kind, j = i % 3, i // 3
        fg = row(final_g) if i == depth - 1 else None
        lname = f"{name}_l{i}"
        if kind == 0 and seq_pool:
            layouts = ("seq" if i == 0 else "time", "seq" if i == depth - 1 else "time")
            xt, s = _pool_layer(xt, None, row(norm_g[i]), pool_in_w[j], pool_w[j], row(pool_scale[j]),
                                pool_out_w[j], fg, offset=offset, name=lname, layouts=layouts)
            new_pool.append(s)
        elif kind == 0:
            st = None if state_pool is None else _to_step_major(state_pool[j])
            xt, s = _pool_layer(xt, st, row(norm_g[i]), pool_in_w[j], pool_w[j], row(pool_scale[j]),
                                pool_out_w[j], fg, offset=offset, name=lname)
            new_pool.append(_to_seq_major(s, POOL_HIST))
        elif kind == 1:
            st = None if state_conv is None else _to_step_major(state_conv[j])
            cw = jnp.broadcast_to(conv_w[j][:, None, :], (CONV_WIDTH, NSEQ, conv_w.shape[2]))
            cb = jnp.broadcast_to(conv_b[j][None, :], (NSEQ, conv_b.shape[1]))
            xt, s = _conv_layer(xt, st, row(norm_g[i]), conv_in_w[j], cw, cb, row(conv_norm_g[j]),
                                row(conv_norm_b[j]), conv_out_w[j], fg, name=lname)
            new_conv.append(_to_seq_major(s, CONV_HIST))
        else:
            xt, s = _sgu_layer(xt, row(norm_g[i]), sgu_in_w[j], row(sgu_norm_g[j]), row(sgu_norm_b[j]),
                               sgu_w[j], sgu_b[j], sgu_out_w[j], fg, name=lname)
            new_sgu.append(s if seq_len >= SGU_CHUNK else _to_seq_major(s, seq_len))
    y = xt if seq_out else xt.transpose(1, 0, 2)
    return y, jnp.stack(new_pool), jnp.stack(new_conv), jnp.stack(new_sgu)


def kernel(x_prompt, x_sample, state_pool, state_conv, norm_g, pool_in_w, pool_w, pool_scale, pool_out_w,
           conv_in_w, conv_w, conv_b, conv_norm_g, conv_norm_b, conv_out_w,
           sgu_in_w, sgu_norm_g, sgu_norm_b, sgu_w, sgu_b, sgu_out_w, final_g):
    past_len = 4096
    weights = (norm_g, pool_in_w.astype(BF16), pool_w.astype(BF16), pool_scale, pool_out_w.astype(BF16),
               conv_in_w.astype(BF16), conv_w, conv_b, conv_norm_g, conv_norm_b, conv_out_w.astype(BF16),
               sgu_in_w.astype(BF16), sgu_norm_g, sgu_norm_b, sgu_w, sgu_b, sgu_out_w.astype(BF16), final_g)
    y_p, pool_p, conv_p, sgu_p = _run_group(x_prompt, None, None, 0, "prompt", *weights)
    y_s, pool_s, conv_s, sgu_s = _run_group(x_sample, state_pool, state_conv, past_len, "sample", *weights)
    return (y_p, y_s, pool_p, pool_s, conv_p, conv_s, sgu_p, sgu_s)
```

```python
import functools

import jax
import jax.numpy as jnp
from jax import lax
from jax.experimental import pallas as pl
from jax.experimental.pallas import tpu as pltpu

POOL_WINDOWS = (2, 4, 8, 16)
POOL_HIST = max(POOL_WINDOWS) - 1
CONV_WIDTH = 31
CONV_HIST = CONV_WIDTH - 1
SGU_CHUNK = 128
SGU_HEADS = 4
RMS_EPS = 1e-6
LN_EPS = 1e-5

NSEQ = 8
LANES = 128
POOL_PAD = 16
CONV_PAD = 32
POOL_TILE_STEPS = 64
POOL_SEQ_TILE = 512
CONV_TILE_STEPS = 32
CONV_ROW_BLOCK = 32
SGU_TILE = 256
PIPE_LAG = 2
VMEM_LIMIT_BYTES = 56 * 1024 * 1024

F32 = jnp.float32
BF16 = jnp.bfloat16


def _silu(v):
    return v * jax.nn.sigmoid(v)


def _rmsnorm_rows(x, g):
    ms = jnp.mean(x * x, axis=-1, keepdims=True)
    return x * lax.rsqrt(ms + RMS_EPS) * g


def _layernorm_rows(x, g, b):
    mu = jnp.mean(x, axis=-1, keepdims=True)
    xc = x - mu
    var = jnp.mean(xc * xc, axis=-1, keepdims=True)
    return xc * lax.rsqrt(var + LN_EPS) * g + b


def _load_rows(x_ref, r0, n):
    if len(x_ref.shape) == 2:
        return x_ref[r0:r0 + n, :]
    return x_ref[r0 // NSEQ:(r0 + n) // NSEQ, :, :].reshape(n, x_ref.shape[2])


def _norm_in(x_ref, g_ref, h_scr, tile):
    rc = 32
    for r0 in range(0, tile, rc):
        h_scr[r0:r0 + rc, :] = _rmsnorm_rows(_load_rows(x_ref, r0, rc), g_ref[...]).astype(BF16)


def _out_proj(xres_ref, m_scr, wout_ref, fg_ref, xo_ref, tile):
    out = jnp.dot(m_scr[...], wout_ref[...], preferred_element_type=F32)
    xn = _load_rows(xres_ref, 0, tile) + out
    if fg_ref is not None:
        xn = _rmsnorm_rows(xn, fg_ref[...])
    xo_ref[...] = xn.reshape(xo_ref.shape)


def _zero_at_start(*scratch):
    @pl.when(pl.program_id(0) == 0)
    def _():
        for ref in scratch:
            ref[...] = jnp.zeros(ref.shape, ref.dtype)


def _pool_kernel(*refs, tile, tps, n_tiles, stride, has_state, has_final):
    refs = list(refs)
    x_ref, xres_ref = refs[:2]
    refs = refs[2:]
    state_ref = refs.pop(0) if has_state else None
    g_ref, win_ref, wgrp_ref, scale_ref, inv_ref, wout_ref = refs[:6]
    refs = refs[6:]
    fg_ref = refs.pop(0) if has_final else None
    xo_ref, st_ref, h_scr, ext_scr, z_scr, sg_scr, d_scr, m_scr = refs

    e = z_scr.shape[1]
    grp = e // len(POOL_WINDOWS)
    base = POOL_PAD * stride
    hist = POOL_HIST * stride
    rc = 32 if stride == NSEQ else 64

    _zero_at_start(ext_scr, z_scr, m_scr)

    _out_proj(xres_ref, m_scr, wout_ref, fg_ref, xo_ref, tile)

    for r0 in range(0, tile, 32):
        sg_scr[r0:r0 + 32, :] = scale_ref[...] * _silu(z_scr[r0:r0 + 32, :])

    _norm_in(x_ref, g_ref, h_scr, tile)
    h = h_scr[...]

    prev = pl.program_id(0) - 1
    sel = jnp.where(prev % tps == 0, 0, 1)
    opens_group = jnp.minimum(pl.program_id(0), n_tiles - 1) % tps == 0

    for gi, w in enumerate(POOL_WINDOWS):
        cols = slice(gi * grp, (gi + 1) * grp)
        for r0 in range(0, tile, rc):
            if stride == NSEQ:
                cur = ext_scr[base + r0:base + r0 + rc, cols]
                acc = cur
                for k in range(1, w):
                    acc = acc + ext_scr[base + r0 - k * NSEQ:base + r0 - k * NSEQ + rc, cols]
            else:
                t = ext_scr[r0:r0 + base + rc, cols]
                cur = t[base:, :]
                sh = 1
                while sh < w:
                    t = t + pltpu.roll(t, sh, 0)
                    sh *= 2
                acc = t[base:, :]
            if r0 < base:
                nt = min(rc, base - r0)
                inv = jnp.tile(inv_ref[sel, r0:r0 + nt, gi * LANES:(gi + 1) * LANES], (1, grp // LANES))
                if nt < rc:
                    inv = jnp.concatenate([inv, jnp.full((rc - nt, grp), 1.0 / w, F32)], axis=0)
                mean = acc * inv
            else:
                mean = acc * (1.0 / w)
            d_scr[r0:r0 + rc, cols] = (mean - cur).astype(BF16)

        if state_ref is not None:
            ext_scr[base - hist:base, cols] = state_ref[:, cols]
        else:
            tail = ext_scr[tile:tile + base, cols]
            ext_scr[0:base, cols] = jnp.where(opens_group, jnp.zeros_like(tail), tail)
        u = jnp.dot(h, win_ref[:, cols], preferred_element_type=F32)
        ext_scr[base:base + tile, cols] = u
        st_ref[:, cols] = u[tile - hist:, :]

        y = jnp.dot(d_scr[:, cols], wgrp_ref[gi], preferred_element_type=F32)
        m_scr[:, cols] = (y * sg_scr[:, cols]).astype(BF16)

    z_scr[...] = jnp.dot(h, win_ref[:, e:2 * e], preferred_element_type=F32)


def _conv_kernel(*refs, tile, tps, has_state, has_final):
    refs = list(refs)
    x_ref, xres_ref = refs[:2]
    refs = refs[2:]
    state_ref = refs.pop(0) if has_state else None
    g_ref, win_ref, wz_ref, cw_ref, cb_ref, ng_ref, nb_ref, wout_ref = refs[:8]
    refs = refs[8:]
    fg_ref = refs.pop(0) if has_final else None
    xo_ref, st_ref, h_scr, ext_scr, sz_scr, y_scr, m_scr = refs

    e = y_scr.shape[1]
    nzb = sz_scr.shape[0]
    base = CONV_PAD * NSEQ
    hist = CONV_HIST * NSEQ
    first = base - hist
    cblk = 512
    rc = CONV_ROW_BLOCK
    nq = rc // NSEQ

    _zero_at_start(y_scr, sz_scr)

    if state_ref is None:
        @pl.when(pl.program_id(0) % tps == 0)
        def _():
            ext_scr[0:base, :] = jnp.zeros((base, e), F32)
    else:
        ext_scr[first:base, :] = state_ref[...]

    _norm_in(x_ref, g_ref, h_scr, tile)

    for r0 in range(0, tile, 16):
        yn = _layernorm_rows(y_scr[r0:r0 + 16, :], ng_ref[...], nb_ref[...])
        sz = jnp.concatenate([sz_scr[j, r0:r0 + 16, :] for j in range(nzb)], axis=1)
        m_scr[r0:r0 + 16, :] = (_silu(yn) * _silu(sz)).astype(BF16)

    h = h_scr[...]
    for c in range(e // cblk):
        cols = slice(c * cblk, (c + 1) * cblk)
        a = jnp.dot(h, win_ref[:, c * cblk:(c + 1) * cblk], preferred_element_type=F32)
        gl = jnp.dot(h, win_ref[:, e + c * cblk:e + (c + 1) * cblk], preferred_element_type=F32)
        ext_scr[base:base + tile, cols] = a * jax.nn.sigmoid(gl)

    _out_proj(xres_ref, m_scr, wout_ref, fg_ref, xo_ref, tile)

    def conv_rows(i, carry):
        r0 = pl.multiple_of(i * rc, rc)
        for c in range(e // cblk):
            cols = slice(c * cblk, (c + 1) * cblk)
            acc = [cb_ref[:, cols] for _ in range(nq)]
            for k in range(CONV_WIDTH):
                wk = cw_ref[k, :, cols]
                for q in range(nq):
                    row = first + r0 + (k + q) * NSEQ
                    acc[q] = acc[q] + wk * ext_scr[pl.ds(row, NSEQ), cols]
            for q in range(nq):
                y_scr[pl.ds(r0 + q * NSEQ, NSEQ), cols] = acc[q]
        sz_scr[i] = jnp.dot(h_scr[...], wz_ref[i], preferred_element_type=F32)
        return carry

    lax.fori_loop(0, tile // rc, conv_rows, 0)

    st_ref[...] = ext_scr[base + tile - hist:base + tile, :]
    if state_ref is None:
        ext_scr[0:base, :] = ext_scr[tile:tile + base, :]


def _sgu_kernel(*refs, tile, tps, has_final):
    del tps
    refs = list(refs)
    x_ref, xres_ref, g_ref, win_ref, ng_ref, nb_ref, ws_ref, bs_ref, wout_ref = refs[:9]
    refs = refs[9:]
    fg_ref = refs.pop(0) if has_final else None
    xo_ref, st_ref, h_scr, u_scr, v_scr, z_scr, uz_scr, vb_scr, m_scr = refs

    e = z_scr.shape[1]
    hd = e // SGU_HEADS
    chunk = ws_ref.shape[1]
    st_rows = st_ref.shape[0]
    rn = 16

    _zero_at_start(u_scr, v_scr, z_scr, m_scr)

    _out_proj(xres_ref, m_scr, wout_ref, fg_ref, xo_ref, tile)

    for r0 in range(0, tile, rn):
        uz_scr[r0:r0 + rn, :] = u_scr[r0:r0 + rn, :] * _silu(z_scr[r0:r0 + rn, :])
        vn = _layernorm_rows(v_scr[r0:r0 + rn, :], ng_ref[...], nb_ref[...])
        vb_scr[r0:r0 + rn, :] = vn.astype(BF16)
        if r0 >= tile - st_rows:
            st_ref[r0 - (tile - st_rows):r0 - (tile - st_rows) + rn, :] = vn

    _norm_in(x_ref, g_ref, h_scr, tile)
    h = h_scr[...]
    u_scr[...] = jnp.dot(h, win_ref[:, 0:e], preferred_element_type=F32)
    v_scr[...] = jnp.dot(h, win_ref[:, e:2 * e], preferred_element_type=F32)
    z_scr[...] = jnp.dot(h, win_ref[:, 2 * e:3 * e], preferred_element_type=F32)

    for c in range(tile // chunk):
        rows = slice(c * chunk, (c + 1) * chunk)
        for hh in range(SGU_HEADS):
            cols = slice(hh * hd, (hh + 1) * hd)
            sp = jnp.dot(ws_ref[hh], vb_scr[rows, cols], preferred_element_type=F32)
            sp = sp + jnp.tile(bs_ref[hh], (1, hd // bs_ref.shape[2]))
            m_scr[rows, cols] = (uz_scr[rows, cols] * sp).astype(BF16)


def _resident(shape):
    nd = len(shape)
    return pl.BlockSpec(shape, lambda s: (0,) * nd, pipeline_mode=pl.Buffered(1))


def _params():
    return pltpu.CompilerParams(dimension_semantics=("arbitrary",), vmem_limit_bytes=VMEM_LIMIT_BYTES)


def _layer_call(kernel_fn, x, x_block, tile_index, n_tiles, tps, state, weights, final_g, st_shape,
                st_stage, scratch, name, lag=PIPE_LAG, out_x=None):
    out_shape_x, out_block, out_index = out_x or (x.shape, x_block, tile_index)
    last = n_tiles - 1
    clamp = lambda t: jnp.clip(t, 0, last)
    operands = [x, x]
    in_specs = [pl.BlockSpec(x_block, lambda s: tile_index(clamp(s))),
                pl.BlockSpec(x_block, lambda s: tile_index(clamp(s - lag)))]
    if state is not None:
        operands.append(state)
        in_specs.append(pl.BlockSpec((None,) + state.shape[1:],
                                     lambda s: (clamp(s) // tps,) + (0,) * (state.ndim - 1)))
    for w in weights:
        operands.append(w)
        in_specs.append(_resident(w.shape))
    if final_g is not None:
        operands.append(final_g)
        in_specs.append(_resident(final_g.shape))
    out_shape = (jax.ShapeDtypeStruct(out_shape_x, F32), jax.ShapeDtypeStruct(st_shape, F32))
    out_specs = (pl.BlockSpec(out_block, lambda s: out_index(clamp(s - lag))),
                 pl.BlockSpec((None,) + st_shape[1:],
                              lambda s: (clamp(s - st_stage) // tps,) + (0,) * (len(st_shape) - 1)))
    return pl.pallas_call(
        functools.partial(kernel_fn, tps=tps),
        grid=(n_tiles + lag,),
        in_specs=in_specs,
        out_specs=out_specs,
        out_shape=out_shape,
        scratch_shapes=scratch,
        compiler_params=_params(),
        name=name,
    )(*operands)


def _group_geometry(x, state, tile_steps):
    steps, nseq, d = x.shape
    tsteps = min(tile_steps, steps)
    tps, groups = steps // tsteps, nseq // NSEQ
    assert state is not None or groups == 1
    assert state is None or tps == 1
    return tsteps * NSEQ, (tsteps, NSEQ, d), (lambda t: (t % tps, t // tps, 0)), tps * groups, tps


def _seq_tiles(layout, steps, nseq, d, tile):
    tps = steps // tile
    if layout == "seq":
        return (nseq, steps, d), (None, tile, d), (lambda t: (t // tps, t % tps, 0))
    return (steps, nseq * d), (tile, d), (lambda t: (t % tps, t // tps))


def _pool_counts(offset, stride):
    pos = offset + jnp.arange(POOL_PAD * stride) // stride
    w = jnp.repeat(jnp.asarray(POOL_WINDOWS), LANES)
    start = 1.0 / jnp.minimum(pos[:, None] + 1, w[None, :]).astype(F32)
    steady = jnp.broadcast_to(1.0 / w.astype(F32), start.shape)
    return jnp.stack([start, steady])


def _pool_layer(x, state, norm_g, in_w, grp_w, scale, out_w, final_g, *, offset, name, layouts=None):
    e = out_w.shape[0]
    if layouts is None:
        d = x.shape[2]
        tile, x_block, tile_index, n_tiles, tps = _group_geometry(x, state, POOL_TILE_STEPS)
        stride, out_x = NSEQ, None
    else:
        nseq, steps, d = x.shape if layouts[0] == "seq" else (x.shape[1], x.shape[0], x.shape[2])
        tile = min(POOL_SEQ_TILE, steps)
        shape, x_block, tile_index = _seq_tiles(layouts[0], steps, nseq, d, tile)
        x = x.reshape(shape)
        out_x = _seq_tiles(layouts[1], steps, nseq, d, tile)
        tps, stride = steps // tile, 1
        n_tiles = tps * nseq
    kern = functools.partial(_pool_kernel, tile=tile, n_tiles=n_tiles, stride=stride,
                             has_state=state is not None, has_final=final_g is not None)
    scratch = [pltpu.VMEM((tile, d), BF16), pltpu.VMEM((POOL_PAD * stride + tile, e), F32),
               pltpu.VMEM((tile, e), F32), pltpu.VMEM((tile, e), F32), pltpu.VMEM((tile, e), BF16),
               pltpu.VMEM((tile, e), BF16)]
    weights = [norm_g, in_w, grp_w, scale, _pool_counts(offset, stride), out_w]
    xo, st = _layer_call(kern, x, x_block, tile_index, n_tiles, tps, state, weights, final_g,
                         (n_tiles // tps, POOL_HIST * stride, e), 0, scratch, name, out_x=out_x)
    if layouts is not None and layouts[1] == "time":
        xo = xo.reshape(steps, nseq, d)
    return xo, st


def _conv_layer(x, state, norm_g, in_w, cw, cb, ng, nb, out_w, final_g, *, name):
    d = x.shape[2]
    e = out_w.shape[0]
    tile, x_block, tile_index, n_tiles, tps = _group_geometry(x, state, CONV_TILE_STEPS)
    kern = functools.partial(_conv_kernel, tile=tile, has_state=state is not None,
                             has_final=final_g is not None)
    nzb = tile // CONV_ROW_BLOCK
    scratch = [pltpu.VMEM((tile, d), BF16), pltpu.VMEM((CONV_PAD * NSEQ + tile, e), F32),
               pltpu.VMEM((nzb, tile, e // nzb), F32), pltpu.VMEM((tile, e), F32), pltpu.VMEM((tile, e), BF16)]
    w_glu = in_w[:, :2 * e]
    w_gate = in_w[:, 2 * e:].reshape(d, nzb, e // nzb).transpose(1, 0, 2)
    return _layer_call(kern, x, x_block, tile_index, n_tiles, tps, state,
                       [norm_g, w_glu, w_gate, cw, cb, ng, nb, out_w], final_g,
                       (n_tiles // tps, CONV_HIST * NSEQ, e), 0, scratch, name, lag=1)


def _sgu_layer(x, norm_g, in_w, ng, nb, ws, bs, out_w, final_g, *, name):
    steps, nseq, d = x.shape
    e = out_w.shape[0]
    if steps >= SGU_CHUNK:
        tile = SGU_TILE
        tps, groups = steps // tile, nseq
        xk = x.reshape(steps, nseq * d)
        x_block, tile_index = (tile, d), (lambda t: (t % tps, t // tps))
        wm = jnp.where(jnp.tril(jnp.ones((SGU_CHUNK, SGU_CHUNK), dtype=bool))[None], ws, jnp.zeros_like(ws))
        bias = bs
        st_rows = SGU_CHUNK
    else:
        tile = steps * NSEQ
        tps, groups = 1, nseq // NSEQ
        xk = x
        x_block, tile_index = (steps, NSEQ, d), (lambda t: (0, t, 0))
        wl = ws[:, :steps, :steps]
        wm = jnp.where(jnp.tril(jnp.ones((steps, steps), dtype=bool))[None], wl, jnp.zeros_like(wl))
        wm = jnp.einsum('hij,ab->hiajb', wm, jnp.eye(NSEQ, dtype=ws.dtype)).reshape(ws.shape[0], tile, tile)
        bias = jnp.repeat(bs[:, :steps], NSEQ, axis=1)
        st_rows = tile
    bias = jnp.broadcast_to(bias[:, :, None], bias.shape + (LANES,))
    kern = functools.partial(_sgu_kernel, tile=tile, has_final=final_g is not None)
    scratch = [pltpu.VMEM((tile, d), BF16), pltpu.VMEM((tile, e), F32), pltpu.VMEM((tile, e), F32),
               pltpu.VMEM((tile, e), F32), pltpu.VMEM((tile, e), F32), pltpu.VMEM((tile, e), BF16),
               pltpu.VMEM((tile, e), BF16)]
    xo, st = _layer_call(kern, xk, x_block, tile_index, tps * groups, tps, None,
                         [norm_g, in_w, ng, nb, wm.astype(BF16), bias, out_w], final_g,
                         (groups, st_rows, e), 1, scratch, name)
    return xo.reshape(steps, nseq, d), st


def _to_step_major(state):
    n, s, e = state.shape
    return state.reshape(n // NSEQ, NSEQ, s, e).transpose(0, 2, 1, 3).reshape(n // NSEQ, s * NSEQ, e)


def _to_seq_major(state, steps):
    g, _, e = state.shape
    return state.reshape(g, steps, NSEQ, e).transpose(0, 2, 1, 3).reshape(g * NSEQ, steps, e)


def _run_group(x, state_pool, state_conv, offset, name, norm_g, pool_in_w, pool_w, pool_scale,
               pool_out_w, conv_in_w, conv_w, conv_b, conv_norm_g, conv_norm_b, conv_out_w,
               sgu_in_w, sgu_norm_g, sgu_norm_b, sgu_w, sgu_b, sgu_out_w, final_g):
    n_seq, seq_len, d = x.shape
    depth = norm_g.shape[0]
    seq_pool = state_pool is None and seq_len >= POOL_SEQ_TILE
    seq_in = seq_pool and depth > 0
    seq_out = seq_pool and (depth - 1) % 3 == 0
    xt = x if seq_in else x.transpose(1, 0, 2)
    row = lambda v: v.reshape(1, -1)
    new_pool, new_conv, new_sgu = [], [], []
    for i in range(depth):
        kind, j = i % 3, i // 3
        fg = row(final_g) if i == depth - 1 else None
        lname = f"{name}_l{i}"
        if kind == 0 and seq_pool:
            layouts = ("seq" if i == 0 else "time", "seq" if i == depth - 1 else "time")
            xt, s = _pool_layer(xt, None, row(norm_g[i]), pool_in_w[j], pool_w[j], row(pool_scale[j]),
                                pool_out_w[j], fg, offset=offset, name=lname, layouts=layouts)
            new_pool.append(s)
        elif kind == 0:
            st = None if state_pool is None else _to_step_major(state_pool[j])
            xt, s = _pool_layer(xt, st, row(norm_g[i]), pool_in_w[j], pool_w[j], row(pool_scale[j]),
                                pool_out_w[j], fg, offset=offset, name=lname)
            new_pool.append(_to_seq_major(s, POOL_HIST))
        elif kind == 1:
            st = None if state_conv is None else _to_step_major(state_conv[j])
            cw = jnp.broadcast_to(conv_w[j][:, None, :], (CONV_WIDTH, NSEQ, conv_w.shape[2]))
            cb = jnp.broadcast_to(conv_b[j][None, :], (NSEQ, conv_b.shape[1]))
            xt, s = _conv_layer(xt, st, row(norm_g[i]), conv_in_w[j], cw, cb, row(conv_norm_g[j]),
                                row(conv_norm_b[j]), conv_out_w[j], fg, name=lname)
            new_conv.append(_to_seq_major(s, CONV_HIST))
        else:
            xt, s = _sgu_layer(xt, row(norm_g[i]), sgu_in_w[j], row(sgu_norm_g[j]), row(sgu_norm_b[j]),
                               sgu_w[j], sgu_b[j], sgu_out_w[j], fg, name=lname)
            new_sgu.append(s if seq_len >= SGU_CHUNK else _to_seq_major(s, seq_len))
    y = xt if seq_out else xt.transpose(1, 0, 2)
    return y, jnp.stack(new_pool), jnp.stack(new_conv), jnp.stack(new_sgu)


def kernel(x_prompt, x_sample, state_pool, state_conv, norm_g, pool_in_w, pool_w, pool_scale, pool_out_w,
           conv_in_w, conv_w, conv_b, conv_norm_g, conv_norm_b, conv_out_w,
           sgu_in_w, sgu_norm_g, sgu_norm_b, sgu_w, sgu_b, sgu_out_w, final_g):
    past_len = 4096
    weights = (norm_g, pool_in_w.astype(BF16), pool_w.astype(BF16), pool_scale, pool_out_w.astype(BF16),
               conv_in_w.astype(BF16), conv_w, conv_b, conv_norm_g, conv_norm_b, conv_out_w.astype(BF16),
               sgu_in_w.astype(BF16), sgu_norm_g, sgu_norm_b, sgu_w, sgu_b, sgu_out_w.astype(BF16), final_g)
    y_p, pool_p, conv_p, sgu_p = _run_group(x_prompt, None, None, 0, "prompt", *weights)
    y_s, pool_s, conv_s, sgu_s = _run_group(x_sample, state_pool, state_conv, past_len, "sample", *weights)
    return (y_p, y_s, pool_p, pool_s, conv_p, conv_s, sgu_p, sgu_s)
```

```python
import functools

import jax
import jax.numpy as jnp
from jax import lax
from jax.experimental import pallas as pl
from jax.experimental.pallas import tpu as pltpu

POOL_WINDOWS = (2, 4, 8, 16)
POOL_HIST = max(POOL_WINDOWS) - 1
CONV_WIDTH = 31
CONV_HIST = CONV_WIDTH - 1
SGU_CHUNK = 128
SGU_HEADS = 4
RMS_EPS = 1e-6
LN_EPS = 1e-5

NSEQ = 8
LANES = 128
POOL_PAD = 16
CONV_PAD = 32
POOL_TILE_STEPS = 64
POOL_SEQ_TILE = 512
CONV_TILE_STEPS = 64
CONV_ROW_BLOCK = 32
SGU_TILE = 256
PIPE_LAG = 2
VMEM_LIMIT_BYTES = 56 * 1024 * 1024
CONV_VMEM_LIMIT_BYTES = 62 * 1024 * 1024

F32 = jnp.float32
BF16 = jnp.bfloat16


def _silu(v):
    return v * jax.nn.sigmoid(v)


def _rmsnorm_rows(x, g):
    ms = jnp.mean(x * x, axis=-1, keepdims=True)
    return x * lax.rsqrt(ms + RMS_EPS) * g


def _layernorm_rows(x, g, b):
    mu = jnp.mean(x, axis=-1, keepdims=True)
    xc = x - mu
    var = jnp.mean(xc * xc, axis=-1, keepdims=True)
    return xc * lax.rsqrt(var + LN_EPS) * g + b


def _load_rows(x_ref, r0, n):
    if len(x_ref.shape) == 2:
        return x_ref[r0:r0 + n, :]
    return x_ref[r0 // NSEQ:(r0 + n) // NSEQ, :, :].reshape(n, x_ref.shape[2])


def _norm_in(x_ref, g_ref, h_scr, tile):
    rc = 32
    for r0 in range(0, tile, rc):
        h_scr[r0:r0 + rc, :] = _rmsnorm_rows(_load_rows(x_ref, r0, rc), g_ref[...]).astype(BF16)


def _out_proj(xres_ref, m_scr, wout_ref, fg_ref, xo_ref, tile):
    out = jnp.dot(m_scr[...], wout_ref[...], preferred_element_type=F32)
    xn = _load_rows(xres_ref, 0, tile) + out
    if fg_ref is not None:
        xn = _rmsnorm_rows(xn, fg_ref[...])
    xo_ref[...] = xn.reshape(xo_ref.shape)


def _zero_at_start(*scratch):
    @pl.when(pl.program_id(0) == 0)
    def _():
        for ref in scratch:
            ref[...] = jnp.zeros(ref.shape, ref.dtype)


def _pool_kernel(*refs, tile, tps, n_tiles, stride, has_state, has_final):
    refs = list(refs)
    x_ref, xres_ref = refs[:2]
    refs = refs[2:]
    state_ref = refs.pop(0) if has_state else None
    g_ref, win_ref, wgrp_ref, scale_ref, inv_ref, wout_ref = refs[:6]
    refs = refs[6:]
    fg_ref = refs.pop(0) if has_final else None
    xo_ref, st_ref, h_scr, ext_scr, z_scr, sg_scr, d_scr, m_scr = refs

    e = z_scr.shape[1]
    grp = e // len(POOL_WINDOWS)
    base = POOL_PAD * stride
    hist = POOL_HIST * stride
    rc = 32 if stride == NSEQ else 64

    _zero_at_start(ext_scr, z_scr, m_scr)

    _out_proj(xres_ref, m_scr, wout_ref, fg_ref, xo_ref, tile)

    for r0 in range(0, tile, 32):
        sg_scr[r0:r0 + 32, :] = scale_ref[...] * _silu(z_scr[r0:r0 + 32, :])

    _norm_in(x_ref, g_ref, h_scr, tile)
    h = h_scr[...]

    prev = pl.program_id(0) - 1
    sel = jnp.where(prev % tps == 0, 0, 1)
    opens_group = jnp.minimum(pl.program_id(0), n_tiles - 1) % tps == 0

    for gi, w in enumerate(POOL_WINDOWS):
        cols = slice(gi * grp, (gi + 1) * grp)
        for r0 in range(0, tile, rc):
            if stride == NSEQ:
                cur = ext_scr[base + r0:base + r0 + rc, cols]
                acc = cur
                for k in range(1, w):
                    acc = acc + ext_scr[base + r0 - k * NSEQ:base + r0 - k * NSEQ + rc, cols]
            else:
                t = ext_scr[r0:r0 + base + rc, cols]
                cur = t[base:, :]
                sh = 1
                while sh < w:
                    t = t + pltpu.roll(t, sh, 0)
                    sh *= 2
                acc = t[base:, :]
            if r0 < base:
                nt = min(rc, base - r0)
                inv = jnp.tile(inv_ref[sel, r0:r0 + nt, gi * LANES:(gi + 1) * LANES], (1, grp // LANES))
                if nt < rc:
                    inv = jnp.concatenate([inv, jnp.full((rc - nt, grp), 1.0 / w, F32)], axis=0)
                mean = acc * inv
            else:
                mean = acc * (1.0 / w)
            d_scr[r0:r0 + rc, cols] = (mean - cur).astype(BF16)

        if state_ref is not None:
            ext_scr[base - hist:base, cols] = state_ref[:, cols]
        else:
            tail = ext_scr[tile:tile + base, cols]
            ext_scr[0:base, cols] = jnp.where(opens_group, jnp.zeros_like(tail), tail)
        u = jnp.dot(h, win_ref[:, cols], preferred_element_type=F32)
        ext_scr[base:base + tile, cols] = u
        st_ref[:, cols] = u[tile - hist:, :]

        y = jnp.dot(d_scr[:, cols], wgrp_ref[gi], preferred_element_type=F32)
        m_scr[:, cols] = (y * sg_scr[:, cols]).astype(BF16)

    z_scr[...] = jnp.dot(h, win_ref[:, e:2 * e], preferred_element_type=F32)


def _conv_kernel(*refs, tile, tps, has_state, has_final):
    refs = list(refs)
    x_ref, xres_ref = refs[:2]
    refs = refs[2:]
    state_ref = refs.pop(0) if has_state else None
    g_ref, win_ref, cw_ref, cb_ref, ng_ref, nb_ref, wout_ref = refs[:7]
    refs = refs[7:]
    fg_ref = refs.pop(0) if has_final else None
    xo_ref, st_ref, h_scr, ext_scr, sz_scr, y_scr, m_scr = refs

    e = sz_scr.shape[1]
    base = CONV_PAD * NSEQ
    hist = CONV_HIST * NSEQ
    first = base - hist
    cblk = 512
    rc = CONV_ROW_BLOCK
    nq = rc // NSEQ

    _zero_at_start(y_scr, sz_scr)

    if state_ref is None:
        @pl.when(pl.program_id(0) % tps == 0)
        def _():
            ext_scr[0:base, :] = jnp.zeros((base, e), F32)
    else:
        ext_scr[first:base, :] = state_ref[...]

    _norm_in(x_ref, g_ref, h_scr, tile)

    for r0 in range(0, tile, 16):
        yn = _layernorm_rows(y_scr[r0:r0 + 16, :], ng_ref[...], nb_ref[...])
        m_scr[r0:r0 + 16, :] = (_silu(yn) * sz_scr[r0:r0 + 16, :]).astype(BF16)

    h = h_scr[...]
    for c in range(e // cblk):
        cols = slice(c * cblk, (c + 1) * cblk)
        a = jnp.dot(h, win_ref[:, c * cblk:(c + 1) * cblk], preferred_element_type=F32)
        gl = jnp.dot(h, win_ref[:, e + c * cblk:e + (c + 1) * cblk], preferred_element_type=F32)
        ext_scr[base:base + tile, cols] = a * jax.nn.sigmoid(gl)
        sz_scr[:, cols] = _silu(jnp.dot(h, win_ref[:, 2 * e + c * cblk:2 * e + (c + 1) * cblk],
                                        preferred_element_type=F32))

    _out_proj(xres_ref, m_scr, wout_ref, fg_ref, xo_ref, tile)

    def conv_rows(i, carry):
        r0 = pl.multiple_of(i * rc, rc)
        for c in range(e // cblk):
            cols = slice(c * cblk, (c + 1) * cblk)
            acc = [cb_ref[:, cols] for _ in range(nq)]
            for k in range(CONV_WIDTH):
                wk = cw_ref[k, :, cols]
                for q in range(nq):
                    row = first + r0 + (k + q) * NSEQ
                    acc[q] = acc[q] + wk * ext_scr[pl.ds(row, NSEQ), cols]
            for q in range(nq):
                y_scr[pl.ds(r0 + q * NSEQ, NSEQ), cols] = acc[q]
        return carry

    lax.fori_loop(0, tile // rc, conv_rows, 0)

    st_ref[...] = ext_scr[base + tile - hist:base + tile, :]
    if state_ref is None:
        ext_scr[0:base, :] = ext_scr[tile:tile + base, :]


def _sgu_kernel(*refs, tile, tps, has_final):
    del tps
    refs = list(refs)
    x_ref, xres_ref, g_ref, win_ref, ng_ref, nb_ref, ws_ref, bs_ref, wout_ref = refs[:9]
    refs = refs[9:]
    fg_ref = refs.pop(0) if has_final else None
    xo_ref, st_ref, h_scr, u_scr, v_scr, z_scr, uz_scr, vb_scr, m_scr = refs

    e = z_scr.shape[1]
    hd = e // SGU_HEADS
    chunk = ws_ref.shape[1]
    st_rows = st_ref.shape[0]
    rn = 16

    _zero_at_start(u_scr, v_scr, z_scr, m_scr)

    _out_proj(xres_ref, m_scr, wout_ref, fg_ref, xo_ref, tile)

    for r0 in range(0, tile, rn):
        uz_scr[r0:r0 + rn, :] = u_scr[r0:r0 + rn, :] * _silu(z_scr[r0:r0 + rn, :])
        vn = _layernorm_rows(v_scr[r0:r0 + rn, :], ng_ref[...], nb_ref[...])
        vb_scr[r0:r0 + rn, :] = vn.astype(BF16)
        if r0 >= tile - st_rows:
            st_ref[r0 - (tile - st_rows):r0 - (tile - st_rows) + rn, :] = vn

    _norm_in(x_ref, g_ref, h_scr, tile)
    h = h_scr[...]
    u_scr[...] = jnp.dot(h, win_ref[:, 0:e], preferred_element_type=F32)
    v_scr[...] = jnp.dot(h, win_ref[:, e:2 * e], preferred_element_type=F32)
    z_scr[...] = jnp.dot(h, win_ref[:, 2 * e:3 * e], preferred_element_type=F32)

    for c in range(tile // chunk):
        rows = slice(c * chunk, (c + 1) * chunk)
        for hh in range(SGU_HEADS):
            cols = slice(hh * hd, (hh + 1) * hd)
            sp = jnp.dot(ws_ref[hh], vb_scr[rows, cols], preferred_element_type=F32)
            sp = sp + jnp.tile(bs_ref[hh], (1, hd // bs_ref.shape[2]))
            m_scr[rows, cols] = (uz_scr[rows, cols] * sp).astype(BF16)


def _resident(shape):
    nd = len(shape)
    return pl.BlockSpec(shape, lambda s: (0,) * nd, pipeline_mode=pl.Buffered(1))


def _params(vmem_limit):
    return pltpu.CompilerParams(dimension_semantics=("arbitrary",), vmem_limit_bytes=vmem_limit)


def _layer_call(kernel_fn, x, x_block, tile_index, n_tiles, tps, state, weights, final_g, st_shape,
                st_stage, scratch, name, lag=PIPE_LAG, out_x=None, vmem_limit=VMEM_LIMIT_BYTES):
    out_shape_x, out_block, out_index = out_x or (x.shape, x_block, tile_index)
    last = n_tiles - 1
    clamp = lambda t: jnp.clip(t, 0, last)
    operands = [x, x]
    in_specs = [pl.BlockSpec(x_block, lambda s: tile_index(clamp(s))),
                pl.BlockSpec(x_block, lambda s: tile_index(clamp(s - lag)))]
    if state is not None:
        operands.append(state)
        in_specs.append(pl.BlockSpec((None,) + state.shape[1:],
                                     lambda s: (clamp(s) // tps,) + (0,) * (state.ndim - 1)))
    for w in weights:
        operands.append(w)
        in_specs.append(_resident(w.shape))
    if final_g is not None:
        operands.append(final_g)
        in_specs.append(_resident(final_g.shape))
    out_shape = (jax.ShapeDtypeStruct(out_shape_x, F32), jax.ShapeDtypeStruct(st_shape, F32))
    out_specs = (pl.BlockSpec(out_block, lambda s: out_index(clamp(s - lag))),
                 pl.BlockSpec((None,) + st_shape[1:],
                              lambda s: (clamp(s - st_stage) // tps,) + (0,) * (len(st_shape) - 1)))
    return pl.pallas_call(
        functools.partial(kernel_fn, tps=tps),
        grid=(n_tiles + lag,),
        in_specs=in_specs,
        out_specs=out_specs,
        out_shape=out_shape,
        scratch_shapes=scratch,
        compiler_params=_params(vmem_limit),
        name=name,
    )(*operands)


def _group_geometry(x, state, tile_steps):
    steps, nseq, d = x.shape
    tsteps = min(tile_steps, steps)
    tps, groups = steps // tsteps, nseq // NSEQ
    assert state is not None or groups == 1
    assert state is None or tps == 1
    return tsteps * NSEQ, (tsteps, NSEQ, d), (lambda t: (t % tps, t // tps, 0)), tps * groups, tps


def _seq_tiles(layout, steps, nseq, d, tile):
    tps = steps // tile
    if layout == "seq":
        return (nseq, steps, d), (None, tile, d), (lambda t: (t // tps, t % tps, 0))
    return (steps, nseq * d), (tile, d), (lambda t: (t % tps, t // tps))


def _pool_counts(offset, stride):
    pos = offset + jnp.arange(POOL_PAD * stride) // stride
    w = jnp.repeat(jnp.asarray(POOL_WINDOWS), LANES)
    start = 1.0 / jnp.minimum(pos[:, None] + 1, w[None, :]).astype(F32)
    steady = jnp.broadcast_to(1.0 / w.astype(F32), start.shape)
    return jnp.stack([start, steady])


def _pool_layer(x, state, norm_g, in_w, grp_w, scale, out_w, final_g, *, offset, name, layouts=None):
    e = out_w.shape[0]
    if layouts is None:
        d = x.shape[2]
        tile, x_block, tile_index, n_tiles, tps = _group_geometry(x, state, POOL_TILE_STEPS)
        stride, out_x = NSEQ, None
    else:
        nseq, steps, d = x.shape if layouts[0] == "seq" else (x.shape[1], x.shape[0], x.shape[2])
        tile = min(POOL_SEQ_TILE, steps)
        shape, x_block, tile_index = _seq_tiles(layouts[0], steps, nseq, d, tile)
        x = x.reshape(shape)
        out_x = _seq_tiles(layouts[1], steps, nseq, d, tile)
        tps, stride = steps // tile, 1
        n_tiles = tps * nseq
    kern = functools.partial(_pool_kernel, tile=tile, n_tiles=n_tiles, stride=stride,
                             has_state=state is not None, has_final=final_g is not None)
    scratch = [pltpu.VMEM((tile, d), BF16), pltpu.VMEM((POOL_PAD * stride + tile, e), F32),
               pltpu.VMEM((tile, e), F32), pltpu.VMEM((tile, e), F32), pltpu.VMEM((tile, e), BF16),
               pltpu.VMEM((tile, e), BF16)]
    weights = [norm_g, in_w, grp_w, scale, _pool_counts(offset, stride), out_w]
    xo, st = _layer_call(kern, x, x_block, tile_index, n_tiles, tps, state, weights, final_g,
                         (n_tiles // tps, POOL_HIST * stride, e), 0, scratch, name, out_x=out_x)
    if layouts is not None and layouts[1] == "time":
        xo = xo.reshape(steps, nseq, d)
    return xo, st


def _conv_layer(x, state, norm_g, in_w, cw, cb, ng, nb, out_w, final_g, *, name):
    d = x.shape[2]
    e = out_w.shape[0]
    tile, x_block, tile_index, n_tiles, tps = _group_geometry(x, state, CONV_TILE_STEPS)
    kern = functools.partial(_conv_kernel, tile=tile, has_state=state is not None,
                             has_final=final_g is not None)
    scratch = [pltpu.VMEM((tile, d), BF16), pltpu.VMEM((CONV_PAD * NSEQ + tile, e), F32),
               pltpu.VMEM((tile, e), F32), pltpu.VMEM((tile, e), F32), pltpu.VMEM((tile, e), BF16)]
    return _layer_call(kern, x, x_block, tile_index, n_tiles, tps, state,
                       [norm_g, in_w, cw, cb, ng, nb, out_w], final_g,
                       (n_tiles // tps, CONV_HIST * NSEQ, e), 0, scratch, name, lag=1,
                       vmem_limit=CONV_VMEM_LIMIT_BYTES)


def _sgu_layer(x, norm_g, in_w, ng, nb, ws, bs, out_w, final_g, *, name):
    steps, nseq, d = x.shape
    e = out_w.shape[0]
    if steps >= SGU_CHUNK:
        tile = SGU_TILE
        tps, groups = steps // tile, nseq
        xk = x.reshape(steps, nseq * d)
        x_block, tile_index = (tile, d), (lambda t: (t % tps, t // tps))
        wm = jnp.where(jnp.tril(jnp.ones((SGU_CHUNK, SGU_CHUNK), dtype=bool))[None], ws, jnp.zeros_like(ws))
        bias = bs
        st_rows = SGU_CHUNK
    else:
        tile = steps * NSEQ
        tps, groups = 1, nseq // NSEQ
        xk = x
        x_block, tile_index = (steps, NSEQ, d), (lambda t: (0, t, 0))
        wl = ws[:, :steps, :steps]
        wm = jnp.where(jnp.tril(jnp.ones((steps, steps), dtype=bool))[None], wl, jnp.zeros_like(wl))
        wm = jnp.einsum('hij,ab->hiajb', wm, jnp.eye(NSEQ, dtype=ws.dtype)).reshape(ws.shape[0], tile, tile)
        bias = jnp.repeat(bs[:, :steps], NSEQ, axis=1)
        st_rows = tile
    bias = jnp.broadcast_to(bias[:, :, None], bias.shape + (LANES,))
    kern = functools.partial(_sgu_kernel, tile=tile, has_final=final_g is not None)
    scratch = [pltpu.VMEM((tile, d), BF16), pltpu.VMEM((tile, e), F32), pltpu.VMEM((tile, e), F32),
               pltpu.VMEM((tile, e), F32), pltpu.VMEM((tile, e), F32), pltpu.VMEM((tile, e), BF16),
               pltpu.VMEM((tile, e), BF16)]
    xo, st = _layer_call(kern, xk, x_block, tile_index, tps * groups, tps, None,
                         [norm_g, in_w, ng, nb, wm.astype(BF16), bias, out_w], final_g,
                         (groups, st_rows, e), 1, scratch, name)
    return xo.reshape(steps, nseq, d), st


def _to_step_major(state):
    n, s, e = state.shape
    return state.reshape(n // NSEQ, NSEQ, s, e).transpose(0, 2, 1, 3).reshape(n // NSEQ, s * NSEQ, e)


def _to_seq_major(state, steps):
    g, _, e = state.shape
    return state.reshape(g, steps, NSEQ, e).transpose(0, 2, 1, 3).reshape(g * NSEQ, steps, e)


def _run_group(x, state_pool, state_conv, offset, name, norm_g, pool_in_w, pool_w, pool_scale,
               pool_out_w, conv_in_w, conv_w, conv_b, conv_norm_g, conv_norm_b, conv_out_w,
               sgu_in_w, sgu_norm_g, sgu_norm_b, sgu_w, sgu_b, sgu_out_w, final_g):
    n_seq, seq_len, d = x.shape
    depth = norm_g.shape[0]
    seq_pool = state_pool is None and seq_len >= POOL_SEQ_TILE
    seq_in = seq_pool and depth > 0
    seq_out = seq_pool and (depth - 1) % 3 == 0
    xt = x if seq_in else x.transpose(1, 0, 2)
    row = lambda v: v.reshape(1, -1)
    new_pool, new_conv, new_sgu = [], [], []
    for i in range(depth):
        kind, j = i % 3, i // 3
        fg = row(final_g) if i == depth - 1 else None
        lname = f"{name}_l{i}"
        if kind == 0 and seq_pool:
            layouts = ("seq" if i == 0 else "time", "seq" if i == depth - 1 else "time")
            xt, s = _pool_layer(xt, None, row(norm_g[i]), pool_in_w[j], pool_w[j], row(pool_scale[j]),
                                pool_out_w[j], fg, offset=offset, name=lname, layouts=layouts)
            new_pool.append(s)
        elif kind == 0:
            st = None if state_pool is None else _to_step_major(state_pool[j])
            xt, s = _pool_layer(xt, st, row(norm_g[i]), pool_in_w[j], pool_w[j], row(pool_scale[j]),
                                pool_out_w[j], fg, offset=offset, name=lname)
            new_pool.append(_to_seq_major(s, POOL_HIST))
        elif kind == 1:
            st = None if state_conv is None else _to_step_major(state_conv[j])
            cw = jnp.broadcast_to(conv_w[j][:, None, :], (CONV_WIDTH, NSEQ, conv_w.shape[2]))
            cb = jnp.broadcast_to(conv_b[j][None, :], (NSEQ, conv_b.shape[1]))
            xt, s = _conv_layer(xt, st, row(norm_g[i]), conv_in_w[j], cw, cb, row(conv_norm_g[j]),
                                row(conv_norm_b[j]), conv_out_w[j], fg, name=lname)
            new_conv.append(_to_seq_major(s, CONV_HIST))
        else:
            xt, s = _sgu_layer(xt, row(norm_g[i]), sgu_in_w[j], row(sgu_norm_g[j]), row(sgu_norm_b[j]),
                               sgu_w[j], sgu_b[j], sgu_out_w[j], fg, name=lname)
            new_sgu.append(s if seq_len >= SGU_CHUNK else _to_seq_major(s, seq_len))
    y = xt if seq_out else xt.transpose(1, 0, 2)
    return y, jnp.stack(new_pool), jnp.stack(new_conv), jnp.stack(new_sgu)


def kernel(x_prompt, x_sample, state_pool, state_conv, norm_g, pool_in_w, pool_w, pool_scale, pool_out_w,
           conv_in_w, conv_w, conv_b, conv_norm_g, conv_norm_b, conv_out_w,
           sgu_in_w, sgu_norm_g, sgu_norm_b, sgu_w, sgu_b, sgu_out_w, final_g):
    past_len = 4096
    weights = (norm_g, pool_in_w.astype(BF16), pool_w.astype(BF16), pool_scale, pool_out_w.astype(BF16),
               conv_in_w.astype(BF16), conv_w, conv_b, conv_norm_g, conv_norm_b, conv_out_w.astype(BF16),
               sgu_in_w.astype(BF16), sgu_norm_g, sgu_norm_b, sgu_w, sgu_b, sgu_out_w.astype(BF16), final_g)
    y_p, pool_p, conv_p, sgu_p = _run_group(x_prompt, None, None, 0, "prompt", *weights)
    y_s, pool_s, conv_s, sgu_s = _run_group(x_sample, state_pool, state_conv, past_len, "sample", *weights)
    return (y_p, y_s, pool_p, pool_s, conv_p, conv_s, sgu_p, sgu_s)
```

```python
import functools

import jax
import jax.numpy as jnp
from jax import lax
from jax.experimental import pallas as pl
from jax.experimental.pallas import tpu as pltpu

POOL_WINDOWS = (2, 4, 8, 16)
POOL_HIST = max(POOL_WINDOWS) - 1
CONV_WIDTH = 31
CONV_HIST = CONV_WIDTH - 1
SGU_CHUNK = 128
SGU_HEADS = 4
RMS_EPS = 1e-6
LN_EPS = 1e-5

NSEQ = 8
LANES = 128
POOL_PAD = 16
CONV_PAD = 32
POOL_TILE_STEPS = 64
POOL_SEQ_TILE = 512
CONV_TILE_STEPS = 64
CONV_ROW_BLOCK = 32
SGU_TILE = 256
PIPE_LAG = 2
VMEM_LIMIT_BYTES = 56 * 1024 * 1024
CONV_VMEM_LIMIT_BYTES = 62 * 1024 * 1024

F32 = jnp.float32
BF16 = jnp.bfloat16


def _silu(v):
    return v * jax.nn.sigmoid(v)


def _rmsnorm_rows(x, g):
    ms = jnp.mean(x * x, axis=-1, keepdims=True)
    return x * lax.rsqrt(ms + RMS_EPS) * g


def _layernorm_rows(x, g, b):
    mu = jnp.mean(x, axis=-1, keepdims=True)
    xc = x - mu
    var = jnp.mean(xc * xc, axis=-1, keepdims=True)
    return xc * lax.rsqrt(var + LN_EPS) * g + b


def _load_rows(x_ref, r0, n):
    if len(x_ref.shape) == 2:
        return x_ref[r0:r0 + n, :]
    return x_ref[r0 // NSEQ:(r0 + n) // NSEQ, :, :].reshape(n, x_ref.shape[2])


def _norm_in(x_ref, g_ref, h_scr, tile):
    rc = 32
    for r0 in range(0, tile, rc):
        h_scr[r0:r0 + rc, :] = _rmsnorm_rows(_load_rows(x_ref, r0, rc), g_ref[...]).astype(BF16)


def _out_proj(xres_ref, m_scr, wout_ref, fg_ref, xo_ref, tile):
    out = jnp.dot(m_scr[...], wout_ref[...], preferred_element_type=F32)
    xn = _load_rows(xres_ref, 0, tile) + out
    if fg_ref is not None:
        xn = _rmsnorm_rows(xn, fg_ref[...])
    xo_ref[...] = xn.reshape(xo_ref.shape)


def _zero_at_start(*scratch):
    @pl.when(pl.program_id(0) == 0)
    def _():
        for ref in scratch:
            ref[...] = jnp.zeros(ref.shape, ref.dtype)


def _pool_kernel(*refs, tile, tps, n_tiles, stride, has_state, has_final):
    refs = list(refs)
    x_ref, xres_ref = refs[:2]
    refs = refs[2:]
    state_ref = refs.pop(0) if has_state else None
    g_ref, win_ref, wgrp_ref, scale_ref, inv_ref, wout_ref = refs[:6]
    refs = refs[6:]
    fg_ref = refs.pop(0) if has_final else None
    xo_ref, st_ref, h_scr, ext_scr, z_scr, sg_scr, d_scr, m_scr = refs

    e = z_scr.shape[1]
    grp = e // len(POOL_WINDOWS)
    base = POOL_PAD * stride
    hist = POOL_HIST * stride
    rc = 32 if stride == NSEQ else 64

    _zero_at_start(ext_scr, z_scr, m_scr)

    _out_proj(xres_ref, m_scr, wout_ref, fg_ref, xo_ref, tile)

    for r0 in range(0, tile, 32):
        sg_scr[r0:r0 + 32, :] = scale_ref[...] * _silu(z_scr[r0:r0 + 32, :])

    _norm_in(x_ref, g_ref, h_scr, tile)
    h = h_scr[...]

    prev = pl.program_id(0) - 1
    sel = jnp.where(prev % tps == 0, 0, 1)
    opens_group = jnp.minimum(pl.program_id(0), n_tiles - 1) % tps == 0

    for gi, w in enumerate(POOL_WINDOWS):
        cols = slice(gi * grp, (gi + 1) * grp)
        for r0 in range(0, tile, rc):
            if stride == NSEQ:
                cur = ext_scr[base + r0:base + r0 + rc, cols]
                acc = cur
                for k in range(1, w):
                    acc = acc + ext_scr[base + r0 - k * NSEQ:base + r0 - k * NSEQ + rc, cols]
            else:
                t = ext_scr[r0:r0 + base + rc, cols]
                cur = t[base:, :]
                sh = 1
                while sh < w:
                    t = t + pltpu.roll(t, sh, 0)
                    sh *= 2
                acc = t[base:, :]
            if r0 < base:
                nt = min(rc, base - r0)
                inv = jnp.tile(inv_ref[sel, r0:r0 + nt, gi * LANES:(gi + 1) * LANES], (1, grp // LANES))
                if nt < rc:
                    inv = jnp.concatenate([inv, jnp.full((rc - nt, grp), 1.0 / w, F32)], axis=0)
                mean = acc * inv
            else:
                mean = acc * (1.0 / w)
            d_scr[r0:r0 + rc, cols] = (mean - cur).astype(BF16)

        if state_ref is not None:
            ext_scr[base - hist:base, cols] = state_ref[:, cols]
        else:
            tail = ext_scr[tile:tile + base, cols]
            ext_scr[0:base, cols] = jnp.where(opens_group, jnp.zeros_like(tail), tail)
        u = jnp.dot(h, win_ref[:, cols], preferred_element_type=F32)
        ext_scr[base:base + tile, cols] = u
        st_ref[:, cols] = u[tile - hist:, :]

        y = jnp.dot(d_scr[:, cols], wgrp_ref[gi], preferred_element_type=F32)
        m_scr[:, cols] = (y * sg_scr[:, cols]).astype(BF16)

    z_scr[...] = jnp.dot(h, win_ref[:, e:2 * e], preferred_element_type=F32)


def _conv_kernel(*refs, tile, tps, has_state, has_final):
    refs = list(refs)
    x_ref, xres_ref = refs[:2]
    refs = refs[2:]
    state_ref = refs.pop(0) if has_state else None
    g_ref, win_ref, cw_ref, cb_ref, ng_ref, nb_ref, wout_ref = refs[:7]
    refs = refs[7:]
    fg_ref = refs.pop(0) if has_final else None
    xo_ref, st_ref, h_scr, ext_scr, sz_scr, y_scr, m_scr = refs

    e = sz_scr.shape[1]
    base = CONV_PAD * NSEQ
    hist = CONV_HIST * NSEQ
    first = base - hist
    cblk = 512
    rc = CONV_ROW_BLOCK
    nq = rc // NSEQ

    _zero_at_start(y_scr, sz_scr)

    if state_ref is None:
        @pl.when(pl.program_id(0) % tps == 0)
        def _():
            ext_scr[0:base, :] = jnp.zeros((base, e), F32)
    else:
        ext_scr[first:base, :] = state_ref[...]

    _norm_in(x_ref, g_ref, h_scr, tile)

    for r0 in range(0, tile, 16):
        yn = _layernorm_rows(y_scr[r0:r0 + 16, :], ng_ref[...], nb_ref[...])
        m_scr[r0:r0 + 16, :] = (_silu(yn) * sz_scr[r0:r0 + 16, :]).astype(BF16)

    h = h_scr[...]
    for c in range(e // cblk):
        cols = slice(c * cblk, (c + 1) * cblk)
        a = jnp.dot(h, win_ref[:, c * cblk:(c + 1) * cblk], preferred_element_type=F32)
        gl = jnp.dot(h, win_ref[:, e + c * cblk:e + (c + 1) * cblk], preferred_element_type=F32)
        ext_scr[base:base + tile, cols] = a * jax.nn.sigmoid(gl)
        sz_scr[:, cols] = _silu(jnp.dot(h, win_ref[:, 2 * e + c * cblk:2 * e + (c + 1) * cblk],
                                        preferred_element_type=F32))

    _out_proj(xres_ref, m_scr, wout_ref, fg_ref, xo_ref, tile)

    def conv_rows(i, carry):
        r0 = pl.multiple_of(i * rc, rc)
        for c in range(e // cblk):
            cols = slice(c * cblk, (c + 1) * cblk)
            acc = [cb_ref[:, cols] for _ in range(nq)]
            for k in range(CONV_WIDTH):
                wk = cw_ref[k, :, cols]
                for q in range(nq):
                    row = first + r0 + (k + q) * NSEQ
                    acc[q] = acc[q] + wk * ext_scr[pl.ds(row, NSEQ), cols]
            for q in range(nq):
                y_scr[pl.ds(r0 + q * NSEQ, NSEQ), cols] = acc[q]
        return carry

    lax.fori_loop(0, tile // rc, conv_rows, 0)

    st_ref[...] = ext_scr[base + tile - hist:base + tile, :]
    if state_ref is None:
        ext_scr[0:base, :] = ext_scr[tile:tile + base, :]


def _sgu_kernel(*refs, tile, tps, has_final):
    del tps
    refs = list(refs)
    x_ref, xres_ref, g_ref, win_ref, ng_ref, nb_ref, ws_ref, bs_ref, wout_ref = refs[:9]
    refs = refs[9:]
    fg_ref = refs.pop(0) if has_final else None
    xo_ref, st_ref, h_scr, u_scr, v_scr, z_scr, uz_scr, vb_scr, m_scr = refs

    e = z_scr.shape[1]
    hd = e // SGU_HEADS
    chunk = ws_ref.shape[1]
    st_rows = st_ref.shape[0]
    rn = 16

    _zero_at_start(u_scr, v_scr, z_scr, m_scr)

    _out_proj(xres_ref, m_scr, wout_ref, fg_ref, xo_ref, tile)

    for r0 in range(0, tile, rn):
        uz_scr[r0:r0 + rn, :] = u_scr[r0:r0 + rn, :] * _silu(z_scr[r0:r0 + rn, :])
        vn = _layernorm_rows(v_scr[r0:r0 + rn, :], ng_ref[...], nb_ref[...])
        vb_scr[r0:r0 + rn, :] = vn.astype(BF16)
        if r0 >= tile - st_rows:
            st_ref[r0 - (tile - st_rows):r0 - (tile - st_rows) + rn, :] = vn

    for c in range(tile // chunk):
        rows = slice(c * chunk, (c + 1) * chunk)
        for hh in range(SGU_HEADS):
            cols = slice(hh * hd, (hh + 1) * hd)
            sp = jnp.dot(ws_ref[hh], vb_scr[rows, cols], preferred_element_type=F32)
            sp = sp + jnp.tile(bs_ref[hh], (1, hd // bs_ref.shape[2]))
            m_scr[rows, cols] = (uz_scr[rows, cols] * sp).astype(BF16)

    _norm_in(x_ref, g_ref, h_scr, tile)
    h = h_scr[...]
    u_scr[...] = jnp.dot(h, win_ref[:, 0:e], preferred_element_type=F32)
    v_scr[...] = jnp.dot(h, win_ref[:, e:2 * e], preferred_element_type=F32)
    z_scr[...] = jnp.dot(h, win_ref[:, 2 * e:3 * e], preferred_element_type=F32)


def _resident(shape):
    nd = len(shape)
    return pl.BlockSpec(shape, lambda s: (0,) * nd, pipeline_mode=pl.Buffered(1))


def _params(vmem_limit):
    return pltpu.CompilerParams(dimension_semantics=("arbitrary",), vmem_limit_bytes=vmem_limit)


def _layer_call(kernel_fn, x, x_block, tile_index, n_tiles, tps, state, weights, final_g, st_shape,
                st_stage, scratch, name, lag=PIPE_LAG, out_x=None, vmem_limit=VMEM_LIMIT_BYTES):
    out_shape_x, out_block, out_index = out_x or (x.shape, x_block, tile_index)
    last = n_tiles - 1
    clamp = lambda t: jnp.clip(t, 0, last)
    operands = [x, x]
    in_specs = [pl.BlockSpec(x_block, lambda s: tile_index(clamp(s))),
                pl.BlockSpec(x_block, lambda s: tile_index(clamp(s - lag)))]
    if state is not None:
        operands.append(state)
        in_specs.append(pl.BlockSpec((None,) + state.shape[1:],
                                     lambda s: (clamp(s) // tps,) + (0,) * (state.ndim - 1)))
    for w in weights:
        operands.append(w)
        in_specs.append(_resident(w.shape))
    if final_g is not None:
        operands.append(final_g)
        in_specs.append(_resident(final_g.shape))
    out_shape = (jax.ShapeDtypeStruct(out_shape_x, F32), jax.ShapeDtypeStruct(st_shape, F32))
    out_specs = (pl.BlockSpec(out_block, lambda s: out_index(clamp(s - lag))),
                 pl.BlockSpec((None,) + st_shape[1:],
                              lambda s: (clamp(s - st_stage) // tps,) + (0,) * (len(st_shape) - 1)))
    return pl.pallas_call(
        functools.partial(kernel_fn, tps=tps),
        grid=(n_tiles + lag,),
        in_specs=in_specs,
        out_specs=out_specs,
        out_shape=out_shape,
        scratch_shapes=scratch,
        compiler_params=_params(vmem_limit),
        name=name,
    )(*operands)


def _group_geometry(x, state, tile_steps):
    steps, nseq, d = x.shape
    tsteps = min(tile_steps, steps)
    tps, groups = steps // tsteps, nseq // NSEQ
    assert state is not None or groups == 1
    assert state is None or tps == 1
    return tsteps * NSEQ, (tsteps, NSEQ, d), (lambda t: (t % tps, t // tps, 0)), tps * groups, tps


def _seq_tiles(layout, steps, nseq, d, tile):
    tps = steps // tile
    if layout == "seq":
        return (nseq, steps, d), (None, tile, d), (lambda t: (t // tps, t % tps, 0))
    return (steps, nseq * d), (tile, d), (lambda t: (t % tps, t // tps))


def _pool_counts(offset, stride):
    pos = offset + jnp.arange(POOL_PAD * stride) // stride
    w = jnp.repeat(jnp.asarray(POOL_WINDOWS), LANES)
    start = 1.0 / jnp.minimum(pos[:, None] + 1, w[None, :]).astype(F32)
    steady = jnp.broadcast_to(1.0 / w.astype(F32), start.shape)
    return jnp.stack([start, steady])


def _pool_layer(x, state, norm_g, in_w, grp_w, scale, out_w, final_g, *, offset, name, layouts=None):
    e = out_w.shape[0]
    if layouts is None:
        d = x.shape[2]
        tile, x_block, tile_index, n_tiles, tps = _group_geometry(x, state, POOL_TILE_STEPS)
        stride, out_x = NSEQ, None
    else:
        nseq, steps, d = x.shape if layouts[0] == "seq" else (x.shape[1], x.shape[0], x.shape[2])
        tile = min(POOL_SEQ_TILE, steps)
        shape, x_block, tile_index = _seq_tiles(layouts[0], steps, nseq, d, tile)
        x = x.reshape(shape)
        out_x = _seq_tiles(layouts[1], steps, nseq, d, tile)
        tps, stride = steps // tile, 1
        n_tiles = tps * nseq
    kern = functools.partial(_pool_kernel, tile=tile, n_tiles=n_tiles, stride=stride,
                             has_state=state is not None, has_final=final_g is not None)
    scratch = [pltpu.VMEM((tile, d), BF16), pltpu.VMEM((POOL_PAD * stride + tile, e), F32),
               pltpu.VMEM((tile, e), F32), pltpu.VMEM((tile, e), F32), pltpu.VMEM((tile, e), BF16),
               pltpu.VMEM((tile, e), BF16)]
    weights = [norm_g, in_w, grp_w, scale, _pool_counts(offset, stride), out_w]
    xo, st = _layer_call(kern, x, x_block, tile_index, n_tiles, tps, state, weights, final_g,
                         (n_tiles // tps, POOL_HIST * stride, e), 0, scratch, name, out_x=out_x)
    if layouts is not None and layouts[1] == "time":
        xo = xo.reshape(steps, nseq, d)
    return xo, st


def _conv_layer(x, state, norm_g, in_w, cw, cb, ng, nb, out_w, final_g, *, name):
    d = x.shape[2]
    e = out_w.shape[0]
    tile, x_block, tile_index, n_tiles, tps = _group_geometry(x, state, CONV_TILE_STEPS)
    kern = functools.partial(_conv_kernel, tile=tile, has_state=state is not None,
                             has_final=final_g is not None)
    scratch = [pltpu.VMEM((tile, d), BF16), pltpu.VMEM((CONV_PAD * NSEQ + tile, e), F32),
               pltpu.VMEM((tile, e), F32), pltpu.VMEM((tile, e), F32), pltpu.VMEM((tile, e), BF16)]
    return _layer_call(kern, x, x_block, tile_index, n_tiles, tps, state,
                       [norm_g, in_w, cw, cb, ng, nb, out_w], final_g,
                       (n_tiles // tps, CONV_HIST * NSEQ, e), 0, scratch, name, lag=1,
                       vmem_limit=CONV_VMEM_LIMIT_BYTES)


def _sgu_layer(x, norm_g, in_w, ng, nb, ws, bs, out_w, final_g, *, name):
    steps, nseq, d = x.shape
    e = out_w.shape[0]
    if steps >= SGU_CHUNK:
        tile = SGU_TILE
        tps, groups = steps // tile, nseq
        xk = x.reshape(steps, nseq * d)
        x_block, tile_index = (tile, d), (lambda t: (t % tps, t // tps))
        wm = jnp.where(jnp.tril(jnp.ones((SGU_CHUNK, SGU_CHUNK), dtype=bool))[None], ws, jnp.zeros_like(ws))
        bias = bs
        st_rows = SGU_CHUNK
    else:
        tile = steps * NSEQ
        tps, groups = 1, nseq // NSEQ
        xk = x
        x_block, tile_index = (steps, NSEQ, d), (lambda t: (0, t, 0))
        wl = ws[:, :steps, :steps]
        wm = jnp.where(jnp.tril(jnp.ones((steps, steps), dtype=bool))[None], wl, jnp.zeros_like(wl))
        wm = jnp.einsum('hij,ab->hiajb', wm, jnp.eye(NSEQ, dtype=ws.dtype)).reshape(ws.shape[0], tile, tile)
        bias = jnp.repeat(bs[:, :steps], NSEQ, axis=1)
        st_rows = tile
    bias = jnp.broadcast_to(bias[:, :, None], bias.shape + (LANES,))
    kern = functools.partial(_sgu_kernel, tile=tile, has_final=final_g is not None)
    scratch = [pltpu.VMEM((tile, d), BF16), pltpu.VMEM((tile, e), F32), pltpu.VMEM((tile, e), F32),
               pltpu.VMEM((tile, e), F32), pltpu.VMEM((tile, e), F32), pltpu.VMEM((tile, e), BF16),
               pltpu.VMEM((tile, e), BF16)]
    xo, st = _layer_call(kern, xk, x_block, tile_index, tps * groups, tps, None,
                         [norm_g, in_w, ng, nb, wm.astype(BF16), bias, out_w], final_g,
                         (groups, st_rows, e), 1, scratch, name)
    return xo.reshape(steps, nseq, d), st


def _to_step_major(state):
    n, s, e = state.shape
    return state.reshape(n // NSEQ, NSEQ, s, e).transpose(0, 2, 1, 3).reshape(n // NSEQ, s * NSEQ, e)


def _to_seq_major(state, steps):
    g, _, e = state.shape
    return state.reshape(g, steps, NSEQ, e).transpose(0, 2, 1, 3).reshape(g * NSEQ, steps, e)


def _run_group(x, state_pool, state_conv, offset, name, norm_g, pool_in_w, pool_w, pool_scale,
               pool_out_w, conv_in_w, conv_w, conv_b, conv_norm_g, conv_norm_b, conv_out_w,
               sgu_in_w, sgu_norm_g, sgu_norm_b, sgu_w, sgu_b, sgu_out_w, final_g):
    n_seq, seq_len, d = x.shape
    depth = norm_g.shape[0]
    seq_pool = state_pool is None and seq_len >= POOL_SEQ_TILE
    seq_in = seq_pool and depth > 0
    seq_out = seq_pool and (depth - 1) % 3 == 0
    xt = x if seq_in else x.transpose(1, 0, 2)
    row = lambda v: v.reshape(1, -1)
    new_pool, new_conv, new_sgu = [], [], []
    for i in range(depth):
        kind, j = i % 3, i // 3
        fg = row(final_g) if i == depth - 1 else None
        lname = f"{name}_l{i}"
        if kind == 0 and seq_pool:
            layouts = ("seq" if i == 0 else "time", "seq" if i == depth - 1 else "time")
            xt, s = _pool_layer(xt, None, row(norm_g[i]), pool_in_w[j], pool_w[j], row(pool_scale[j]),
                                pool_out_w[j], fg, offset=offset, name=lname, layouts=layouts)
            new_pool.append(s)
        elif kind == 0:
            st = None if state_pool is None else _to_step_major(state_pool[j])
            xt, s = _pool_layer(xt, st, row(norm_g[i]), pool_in_w[j], pool_w[j], row(pool_scale[j]),
                                pool_out_w[j], fg, offset=offset, name=lname)
            new_pool.append(_to_seq_major(s, POOL_HIST))
        elif kind == 1:
            st = None if state_conv is None else _to_step_major(state_conv[j])
            cw = jnp.broadcast_to(conv_w[j][:, None, :], (CONV_WIDTH, NSEQ, conv_w.shape[2]))
            cb = jnp.broadcast_to(conv_b[j][None, :], (NSEQ, conv_b.shape[1]))
            xt, s = _conv_layer(xt, st, row(norm_g[i]), conv_in_w[j], cw, cb, row(conv_norm_g[j]),
                                row(conv_norm_b[j]), conv_out_w[j], fg, name=lname)
            new_conv.append(_to_seq_major(s, CONV_HIST))
        else:
            xt, s = _sgu_layer(xt, row(norm_g[i]), sgu_in_w[j], row(sgu_norm_g[j]), row(sgu_norm_b[j]),
                               sgu_w[j], sgu_b[j], sgu_out_w[j], fg, name=lname)
            new_sgu.append(s if seq_len >= SGU_CHUNK else _to_seq_major(s, seq_len))
    y = xt if seq_out else xt.transpose(1, 0, 2)
    return y, jnp.stack(new_pool), jnp.stack(new_conv), jnp.stack(new_sgu)


def kernel(x_prompt, x_sample, state_pool, state_conv, norm_g, pool_in_w, pool_w, pool_scale, pool_out_w,
           conv_in_w, conv_w, conv_b, conv_norm_g, conv_norm_b, conv_out_w,
           sgu_in_w, sgu_norm_g, sgu_norm_b, sgu_w, sgu_b, sgu_out_w, final_g):
    past_len = 4096
    weights = (norm_g, pool_in_w.astype(BF16), pool_w.astype(BF16), pool_scale, pool_out_w.astype(BF16),
               conv_in_w.astype(BF16), conv_w, conv_b, conv_norm_g, conv_norm_b, conv_out_w.astype(BF16),
               sgu_in_w.astype(BF16), sgu_norm_g, sgu_norm_b, sgu_w, sgu_b, sgu_out_w.astype(BF16), final_g)
    y_p, pool_p, conv_p, sgu_p = _run_group(x_prompt, None, None, 0, "prompt", *weights)
    y_s, pool_s, conv_s, sgu_s = _run_group(x_sample, state_pool, state_conv, past_len, "sample", *weights)
    return (y_p, y_s, pool_p, pool_s, conv_p, conv_s, sgu_p, sgu_s)
```

```python
import functools

import jax
import jax.numpy as jnp
from jax import lax
from jax.experimental import pallas as pl
from jax.experimental.pallas import tpu as pltpu

POOL_WINDOWS = (2, 4, 8, 16)
POOL_HIST = max(POOL_WINDOWS) - 1
CONV_WIDTH = 31
CONV_HIST = CONV_WIDTH - 1
SGU_CHUNK = 128
SGU_HEADS = 4
RMS_EPS = 1e-6
LN_EPS = 1e-5

NSEQ = 8
LANES = 128
POOL_PAD = 16
CONV_PAD = 32
POOL_TILE_STEPS = 64
POOL_SEQ_TILE = 512
CONV_TILE_STEPS = 64
CONV_ROW_BLOCK = 32
SGU_TILE = 256
PIPE_LAG = 2
VMEM_LIMIT_BYTES = 56 * 1024 * 1024
CONV_VMEM_LIMIT_BYTES = 62 * 1024 * 1024

F32 = jnp.float32
BF16 = jnp.bfloat16


def _silu(v):
    return v * jax.nn.sigmoid(v)


def _rmsnorm_rows(x, g):
    ms = jnp.mean(x * x, axis=-1, keepdims=True)
    return x * lax.rsqrt(ms + RMS_EPS) * g


def _layernorm_rows(x, g, b):
    mu = jnp.mean(x, axis=-1, keepdims=True)
    xc = x - mu
    var = jnp.mean(xc * xc, axis=-1, keepdims=True)
    return xc * lax.rsqrt(var + LN_EPS) * g + b


def _load_rows(x_ref, r0, n):
    if len(x_ref.shape) == 2:
        return x_ref[r0:r0 + n, :]
    ns = x_ref.shape[1]
    return x_ref[r0 // ns:(r0 + n) // ns, :, :].reshape(n, x_ref.shape[2])


def _norm_in(x_ref, g_ref, h_scr, tile):
    rc = 32
    for r0 in range(0, tile, rc):
        h_scr[r0:r0 + rc, :] = _rmsnorm_rows(_load_rows(x_ref, r0, rc), g_ref[...]).astype(BF16)


def _out_proj(xres_ref, m_scr, wout_ref, fg_ref, xo_ref, tile):
    out = jnp.dot(m_scr[...], wout_ref[...], preferred_element_type=F32)
    xn = _load_rows(xres_ref, 0, tile) + out
    if fg_ref is not None:
        xn = _rmsnorm_rows(xn, fg_ref[...])
    xo_ref[...] = xn.reshape(xo_ref.shape)


def _zero_at_start(*scratch):
    @pl.when(pl.program_id(0) == 0)
    def _():
        for ref in scratch:
            ref[...] = jnp.zeros(ref.shape, ref.dtype)


def _pool_kernel(*refs, tile, tps, n_tiles, stride, has_state, has_final):
    refs = list(refs)
    x_ref, xres_ref = refs[:2]
    refs = refs[2:]
    state_ref = refs.pop(0) if has_state else None
    g_ref, win_ref, wgrp_ref, scale_ref, inv_ref, wout_ref = refs[:6]
    refs = refs[6:]
    fg_ref = refs.pop(0) if has_final else None
    xo_ref, st_ref, h_scr, ext_scr, z_scr, sg_scr, d_scr, m_scr = refs

    e = z_scr.shape[1]
    grp = e // len(POOL_WINDOWS)
    base = POOL_PAD * stride
    hist = POOL_HIST * stride
    rc = 32 if stride > 1 else 64

    _zero_at_start(ext_scr, z_scr, m_scr)

    _out_proj(xres_ref, m_scr, wout_ref, fg_ref, xo_ref, tile)

    for r0 in range(0, tile, 32):
        sg_scr[r0:r0 + 32, :] = scale_ref[...] * _silu(z_scr[r0:r0 + 32, :])

    _norm_in(x_ref, g_ref, h_scr, tile)
    h = h_scr[...]

    prev = pl.program_id(0) - 1
    sel = jnp.where(prev % tps == 0, 0, 1)
    opens_group = jnp.minimum(pl.program_id(0), n_tiles - 1) % tps == 0

    for gi, w in enumerate(POOL_WINDOWS):
        cols = slice(gi * grp, (gi + 1) * grp)
        for r0 in range(0, tile, rc):
            if stride > 1:
                cur = ext_scr[base + r0:base + r0 + rc, cols]
                acc = cur
                for k in range(1, w):
                    acc = acc + ext_scr[base + r0 - k * stride:base + r0 - k * stride + rc, cols]
            else:
                t = ext_scr[r0:r0 + base + rc, cols]
                cur = t[base:, :]
                sh = 1
                while sh < w:
                    t = t + pltpu.roll(t, sh, 0)
                    sh *= 2
                acc = t[base:, :]
            if r0 < base:
                nt = min(rc, base - r0)
                inv = jnp.tile(inv_ref[sel, r0:r0 + nt, gi * LANES:(gi + 1) * LANES], (1, grp // LANES))
                if nt < rc:
                    inv = jnp.concatenate([inv, jnp.full((rc - nt, grp), 1.0 / w, F32)], axis=0)
                mean = acc * inv
            else:
                mean = acc * (1.0 / w)
            d_scr[r0:r0 + rc, cols] = (mean - cur).astype(BF16)

        if state_ref is not None:
            ext_scr[base - hist:base, cols] = state_ref[:, cols]
        else:
            tail = ext_scr[tile:tile + base, cols]
            ext_scr[0:base, cols] = jnp.where(opens_group, jnp.zeros_like(tail), tail)
        u = jnp.dot(h, win_ref[:, cols], preferred_element_type=F32)
        ext_scr[base:base + tile, cols] = u
        st_ref[:, cols] = u[tile - hist:, :]

        y = jnp.dot(d_scr[:, cols], wgrp_ref[gi], preferred_element_type=F32)
        m_scr[:, cols] = (y * sg_scr[:, cols]).astype(BF16)

    z_scr[...] = jnp.dot(h, win_ref[:, e:2 * e], preferred_element_type=F32)


def _conv_kernel(*refs, tile, tps, has_state, has_final):
    refs = list(refs)
    x_ref, xres_ref = refs[:2]
    refs = refs[2:]
    state_ref = refs.pop(0) if has_state else None
    g_ref, win_ref, cw_ref, cb_ref, ng_ref, nb_ref, wout_ref = refs[:7]
    refs = refs[7:]
    fg_ref = refs.pop(0) if has_final else None
    xo_ref, st_ref, h_scr, ext_scr, sz_scr, y_scr, m_scr = refs

    e = sz_scr.shape[1]
    base = CONV_PAD * NSEQ
    hist = CONV_HIST * NSEQ
    first = base - hist
    cblk = 512
    rc = CONV_ROW_BLOCK
    nq = rc // NSEQ

    _zero_at_start(y_scr, sz_scr)

    if state_ref is None:
        @pl.when(pl.program_id(0) % tps == 0)
        def _():
            ext_scr[0:base, :] = jnp.zeros((base, e), F32)
    else:
        ext_scr[first:base, :] = state_ref[...]

    _norm_in(x_ref, g_ref, h_scr, tile)

    for r0 in range(0, tile, 16):
        yn = _layernorm_rows(y_scr[r0:r0 + 16, :], ng_ref[...], nb_ref[...])
        m_scr[r0:r0 + 16, :] = (_silu(yn) * sz_scr[r0:r0 + 16, :]).astype(BF16)

    h = h_scr[...]
    for c in range(e // cblk):
        cols = slice(c * cblk, (c + 1) * cblk)
        a = jnp.dot(h, win_ref[:, c * cblk:(c + 1) * cblk], preferred_element_type=F32)
        gl = jnp.dot(h, win_ref[:, e + c * cblk:e + (c + 1) * cblk], preferred_element_type=F32)
        ext_scr[base:base + tile, cols] = a * jax.nn.sigmoid(gl)
        sz_scr[:, cols] = _silu(jnp.dot(h, win_ref[:, 2 * e + c * cblk:2 * e + (c + 1) * cblk],
                                        preferred_element_type=F32))

    _out_proj(xres_ref, m_scr, wout_ref, fg_ref, xo_ref, tile)

    def conv_rows(i, carry):
        r0 = pl.multiple_of(i * rc, rc)
        for c in range(e // cblk):
            cols = slice(c * cblk, (c + 1) * cblk)
            acc = [cb_ref[:, cols] for _ in range(nq)]
            for k in range(CONV_WIDTH):
                wk = cw_ref[k, :, cols]
                for q in range(nq):
                    row = first + r0 + (k + q) * NSEQ
                    acc[q] = acc[q] + wk * ext_scr[pl.ds(row, NSEQ), cols]
            for q in range(nq):
                y_scr[pl.ds(r0 + q * NSEQ, NSEQ), cols] = acc[q]
        return carry

    lax.fori_loop(0, tile // rc, conv_rows, 0)

    st_ref[...] = ext_scr[base + tile - hist:base + tile, :]
    if state_ref is None:
        ext_scr[0:base, :] = ext_scr[tile:tile + base, :]


def _sgu_kernel(*refs, tile, tps, has_final):
    del tps
    refs = list(refs)
    x_ref, xres_ref, g_ref, win_ref, ng_ref, nb_ref, ws_ref, bs_ref, wout_ref = refs[:9]
    refs = refs[9:]
    fg_ref = refs.pop(0) if has_final else None
    xo_ref, st_ref, h_scr, u_scr, v_scr, z_scr, uz_scr, vb_scr, m_scr = refs

    e = z_scr.shape[1]
    hd = e // SGU_HEADS
    chunk = ws_ref.shape[1]
    st_rows = st_ref.shape[0]
    rn = 16

    _zero_at_start(u_scr, v_scr, z_scr, m_scr)

    _out_proj(xres_ref, m_scr, wout_ref, fg_ref, xo_ref, tile)

    for r0 in range(0, tile, rn):
        uz_scr[r0:r0 + rn, :] = u_scr[r0:r0 + rn, :] * _silu(z_scr[r0:r0 + rn, :])
        vn = _layernorm_rows(v_scr[r0:r0 + rn, :], ng_ref[...], nb_ref[...])
        vb_scr[r0:r0 + rn, :] = vn.astype(BF16)
        if r0 >= tile - st_rows:
            st_ref[r0 - (tile - st_rows):r0 - (tile - st_rows) + rn, :] = vn

    for c in range(tile // chunk):
        rows = slice(c * chunk, (c + 1) * chunk)
        for hh in range(SGU_HEADS):
            cols = slice(hh * hd, (hh + 1) * hd)
            sp = jnp.dot(ws_ref[hh], vb_scr[rows, cols], preferred_element_type=F32)
            sp = sp + jnp.tile(bs_ref[hh], (1, hd // bs_ref.shape[2]))
            m_scr[rows, cols] = (uz_scr[rows, cols] * sp).astype(BF16)

    _norm_in(x_ref, g_ref, h_scr, tile)
    h = h_scr[...]
    u_scr[...] = jnp.dot(h, win_ref[:, 0:e], preferred_element_type=F32)
    v_scr[...] = jnp.dot(h, win_ref[:, e:2 * e], preferred_element_type=F32)
    z_scr[...] = jnp.dot(h, win_ref[:, 2 * e:3 * e], preferred_element_type=F32)


def _resident(shape):
    nd = len(shape)
    return pl.BlockSpec(shape, lambda s: (0,) * nd, pipeline_mode=pl.Buffered(1))


def _params(vmem_limit):
    return pltpu.CompilerParams(dimension_semantics=("arbitrary",), vmem_limit_bytes=vmem_limit)


def _layer_call(kernel_fn, x, x_block, tile_index, n_tiles, tps, state, weights, final_g, st_shape,
                st_stage, scratch, name, lag=PIPE_LAG, out_x=None, vmem_limit=VMEM_LIMIT_BYTES):
    out_shape_x, out_block, out_index = out_x or (x.shape, x_block, tile_index)
    last = n_tiles - 1
    clamp = lambda t: jnp.clip(t, 0, last)
    operands = [x, x]
    in_specs = [pl.BlockSpec(x_block, lambda s: tile_index(clamp(s))),
                pl.BlockSpec(x_block, lambda s: tile_index(clamp(s - lag)))]
    if state is not None:
        operands.append(state)
        in_specs.append(pl.BlockSpec((None,) + state.shape[1:],
                                     lambda s: (clamp(s) // tps,) + (0,) * (state.ndim - 1)))
    for w in weights:
        operands.append(w)
        in_specs.append(_resident(w.shape))
    if final_g is not None:
        operands.append(final_g)
        in_specs.append(_resident(final_g.shape))
    out_shape = (jax.ShapeDtypeStruct(out_shape_x, F32), jax.ShapeDtypeStruct(st_shape, F32))
    out_specs = (pl.BlockSpec(out_block, lambda s: out_index(clamp(s - lag))),
                 pl.BlockSpec((None,) + st_shape[1:],
                              lambda s: (clamp(s - st_stage) // tps,) + (0,) * (len(st_shape) - 1)))
    return pl.pallas_call(
        functools.partial(kernel_fn, tps=tps),
        grid=(n_tiles + lag,),
        in_specs=in_specs,
        out_specs=out_specs,
        out_shape=out_shape,
        scratch_shapes=scratch,
        compiler_params=_params(vmem_limit),
        name=name,
    )(*operands)


def _group_geometry(x, state, tile_steps, seqs=NSEQ):
    steps, nseq, d = x.shape
    tsteps = min(tile_steps, steps)
    tps, groups = steps // tsteps, nseq // seqs
    assert state is not None or groups == 1
    assert state is None or tps == 1
    return tsteps * seqs, (tsteps, seqs, d), (lambda t: (t % tps, t // tps, 0)), tps * groups, tps


def _seq_tiles(layout, steps, nseq, d, tile):
    tps = steps // tile
    if layout == "seq":
        return (nseq, steps, d), (None, tile, d), (lambda t: (t // tps, t % tps, 0))
    return (steps, nseq * d), (tile, d), (lambda t: (t % tps, t // tps))


def _pool_seqs(x):
    steps, nseq, _ = x.shape
    seqs = NSEQ
    while seqs * 2 * min(POOL_TILE_STEPS, steps) <= POOL_SEQ_TILE and nseq % (seqs * 2) == 0:
        seqs *= 2
    return seqs


def _pool_counts(offset, stride):
    pos = offset + jnp.arange(POOL_PAD * stride) // stride
    w = jnp.repeat(jnp.asarray(POOL_WINDOWS), LANES)
    start = 1.0 / jnp.minimum(pos[:, None] + 1, w[None, :]).astype(F32)
    steady = jnp.broadcast_to(1.0 / w.astype(F32), start.shape)
    return jnp.stack([start, steady])


def _pool_layer(x, state, norm_g, in_w, grp_w, scale, out_w, final_g, *, offset, name, layouts=None):
    e = out_w.shape[0]
    if layouts is None:
        d = x.shape[2]
        stride, out_x = _pool_seqs(x), None
        tile, x_block, tile_index, n_tiles, tps = _group_geometry(x, state, POOL_TILE_STEPS, stride)
    else:
        nseq, steps, d = x.shape if layouts[0] == "seq" else (x.shape[1], x.shape[0], x.shape[2])
        tile = min(POOL_SEQ_TILE, steps)
        shape, x_block, tile_index = _seq_tiles(layouts[0], steps, nseq, d, tile)
        x = x.reshape(shape)
        out_x = _seq_tiles(layouts[1], steps, nseq, d, tile)
        tps, stride = steps // tile, 1
        n_tiles = tps * nseq
    kern = functools.partial(_pool_kernel, tile=tile, n_tiles=n_tiles, stride=stride,
                             has_state=state is not None, has_final=final_g is not None)
    scratch = [pltpu.VMEM((tile, d), BF16), pltpu.VMEM((POOL_PAD * stride + tile, e), F32),
               pltpu.VMEM((tile, e), F32), pltpu.VMEM((tile, e), F32), pltpu.VMEM((tile, e), BF16),
               pltpu.VMEM((tile, e), BF16)]
    weights = [norm_g, in_w, grp_w, scale, _pool_counts(offset, stride), out_w]
    xo, st = _layer_call(kern, x, x_block, tile_index, n_tiles, tps, state, weights, final_g,
                         (n_tiles // tps, POOL_HIST * stride, e), 0, scratch, name, out_x=out_x)
    if layouts is not None and layouts[1] == "time":
        xo = xo.reshape(steps, nseq, d)
    return xo, st


def _conv_layer(x, state, norm_g, in_w, cw, cb, ng, nb, out_w, final_g, *, name):
    d = x.shape[2]
    e = out_w.shape[0]
    tile, x_block, tile_index, n_tiles, tps = _group_geometry(x, state, CONV_TILE_STEPS)
    kern = functools.partial(_conv_kernel, tile=tile, has_state=state is not None,
                             has_final=final_g is not None)
    scratch = [pltpu.VMEM((tile, d), BF16), pltpu.VMEM((CONV_PAD * NSEQ + tile, e), F32),
               pltpu.VMEM((tile, e), F32), pltpu.VMEM((tile, e), F32), pltpu.VMEM((tile, e), BF16)]
    return _layer_call(kern, x, x_block, tile_index, n_tiles, tps, state,
                       [norm_g, in_w, cw, cb, ng, nb, out_w], final_g,
                       (n_tiles // tps, CONV_HIST * NSEQ, e), 0, scratch, name, lag=1,
                       vmem_limit=CONV_VMEM_LIMIT_BYTES)


def _sgu_layer(x, norm_g, in_w, ng, nb, ws, bs, out_w, final_g, *, name):
    steps, nseq, d = x.shape
    e = out_w.shape[0]
    if steps >= SGU_CHUNK:
        tile = SGU_TILE
        tps, groups = steps // tile, nseq
        xk = x.reshape(steps, nseq * d)
        x_block, tile_index = (tile, d), (lambda t: (t % tps, t // tps))
        wm = jnp.where(jnp.tril(jnp.ones((SGU_CHUNK, SGU_CHUNK), dtype=bool))[None], ws, jnp.zeros_like(ws))
        bias = bs
        st_rows = SGU_CHUNK
    else:
        tile = steps * NSEQ
        tps, groups = 1, nseq // NSEQ
        xk = x
        x_block, tile_index = (steps, NSEQ, d), (lambda t: (0, t, 0))
        wl = ws[:, :steps, :steps]
        wm = jnp.where(jnp.tril(jnp.ones((steps, steps), dtype=bool))[None], wl, jnp.zeros_like(wl))
        wm = jnp.einsum('hij,ab->hiajb', wm, jnp.eye(NSEQ, dtype=ws.dtype)).reshape(ws.shape[0], tile, tile)
        bias = jnp.repeat(bs[:, :steps], NSEQ, axis=1)
        st_rows = tile
    bias = jnp.broadcast_to(bias[:, :, None], bias.shape + (LANES,))
    kern = functools.partial(_sgu_kernel, tile=tile, has_final=final_g is not None)
    scratch = [pltpu.VMEM((tile, d), BF16), pltpu.VMEM((tile, e), F32), pltpu.VMEM((tile, e), F32),
               pltpu.VMEM((tile, e), F32), pltpu.VMEM((tile, e), F32), pltpu.VMEM((tile, e), BF16),
               pltpu.VMEM((tile, e), BF16)]
    xo, st = _layer_call(kern, xk, x_block, tile_index, tps * groups, tps, None,
                         [norm_g, in_w, ng, nb, wm.astype(BF16), bias, out_w], final_g,
                         (groups, st_rows, e), 1, scratch, name)
    return xo.reshape(steps, nseq, d), st


def _to_step_major(state, seqs=NSEQ):
    n, s, e = state.shape
    return state.reshape(n // seqs, seqs, s, e).transpose(0, 2, 1, 3).reshape(n // seqs, s * seqs, e)


def _to_seq_major(state, steps, seqs=NSEQ):
    g, _, e = state.shape
    return state.reshape(g, steps, seqs, e).transpose(0, 2, 1, 3).reshape(g * seqs, steps, e)


def _run_group(x, state_pool, state_conv, offset, name, norm_g, pool_in_w, pool_w, pool_scale,
               pool_out_w, conv_in_w, conv_w, conv_b, conv_norm_g, conv_norm_b, conv_out_w,
               sgu_in_w, sgu_norm_g, sgu_norm_b, sgu_w, sgu_b, sgu_out_w, final_g):
    n_seq, seq_len, d = x.shape
    depth = norm_g.shape[0]
    seq_pool = state_pool is None and seq_len >= POOL_SEQ_TILE
    seq_in = seq_pool and depth > 0
    seq_out = seq_pool and (depth - 1) % 3 == 0
    xt = x if seq_in else x.transpose(1, 0, 2)
    row = lambda v: v.reshape(1, -1)
    new_pool, new_conv, new_sgu = [], [], []
    for i in range(depth):
        kind, j = i % 3, i // 3
        fg = row(final_g) if i == depth - 1 else None
        lname = f"{name}_l{i}"
        if kind == 0 and seq_pool:
            layouts = ("seq" if i == 0 else "time", "seq" if i == depth - 1 else "time")
            xt, s = _pool_layer(xt, None, row(norm_g[i]), pool_in_w[j], pool_w[j], row(pool_scale[j]),
                                pool_out_w[j], fg, offset=offset, name=lname, layouts=layouts)
            new_pool.append(s)
        elif kind == 0:
            seqs = _pool_seqs(xt)
            st = None if state_pool is None else _to_step_major(state_pool[j], seqs)
            xt, s = _pool_layer(xt, st, row(norm_g[i]), pool_in_w[j], pool_w[j], row(pool_scale[j]),
                                pool_out_w[j], fg, offset=offset, name=lname)
            new_pool.append(_to_seq_major(s, POOL_HIST, seqs))
        elif kind == 1:
            st = None if state_conv is None else _to_step_major(state_conv[j])
            cw = jnp.broadcast_to(conv_w[j][:, None, :], (CONV_WIDTH, NSEQ, conv_w.shape[2]))
            cb = jnp.broadcast_to(conv_b[j][None, :], (NSEQ, conv_b.shape[1]))
            xt, s = _conv_layer(xt, st, row(norm_g[i]), conv_in_w[j], cw, cb, row(conv_norm_g[j]),
                                row(conv_norm_b[j]), conv_out_w[j], fg, name=lname)
            new_conv.append(_to_seq_major(s, CONV_HIST))
        else:
            xt, s = _sgu_layer(xt, row(norm_g[i]), sgu_in_w[j], row(sgu_norm_g[j]), row(sgu_norm_b[j]),
                               sgu_w[j], sgu_b[j], sgu_out_w[j], fg, name=lname)
            new_sgu.append(s if seq_len >= SGU_CHUNK else _to_seq_major(s, seq_len))
    y = xt if seq_out else xt.transpose(1, 0, 2)
    return y, jnp.stack(new_pool), jnp.stack(new_conv), jnp.stack(new_sgu)


def kernel(x_prompt, x_sample, state_pool, state_conv, norm_g, pool_in_w, pool_w, pool_scale, pool_out_w,
           conv_in_w, conv_w, conv_b, conv_norm_g, conv_norm_b, conv_out_w,
           sgu_in_w, sgu_norm_g, sgu_norm_b, sgu_w, sgu_b, sgu_out_w, final_g):
    past_len = 4096
    weights = (norm_g, pool_in_w.astype(BF16), pool_w.astype(BF16), pool_scale, pool_out_w.astype(BF16),
               conv_in_w.astype(BF16), conv_w, conv_b, conv_norm_g, conv_norm_b, conv_out_w.astype(BF16),
               sgu_in_w.astype(BF16), sgu_norm_g, sgu_norm_b, sgu_w, sgu_b, sgu_out_w.astype(BF16), final_g)
    y_p, pool_p, conv_p, sgu_p = _run_group(x_prompt, None, None, 0, "prompt", *weights)
    y_s, pool_s, conv_s, sgu_s = _run_group(x_sample, state_pool, state_conv, past_len, "sample", *weights)
    return (y_p, y_s, pool_p, pool_s, conv_p, conv_s, sgu_p, sgu_s)
```

```python
import functools

import jax
import jax.numpy as jnp
from jax import lax
from jax.experimental import pallas as pl
from jax.experimental.pallas import tpu as pltpu

POOL_WINDOWS = (2, 4, 8, 16)
POOL_HIST = max(POOL_WINDOWS) - 1
CONV_WIDTH = 31
CONV_HIST = CONV_WIDTH - 1
SGU_CHUNK = 128
SGU_HEADS = 4
RMS_EPS = 1e-6
LN_EPS = 1e-5

NSEQ = 8
LANES = 128
POOL_PAD = 16
CONV_PAD = 32
POOL_TILE_STEPS = 64
POOL_SEQ_TILE = 512
CONV_TILE_STEPS = 64
CONV_ROW_BLOCK = 32
SGU_TILE = 256
PIPE_LAG = 2
VMEM_LIMIT_BYTES = 56 * 1024 * 1024
CONV_VMEM_LIMIT_BYTES = 62 * 1024 * 1024

F32 = jnp.float32
BF16 = jnp.bfloat16


def _silu(v):
    return v * jax.nn.sigmoid(v)


def _rmsnorm_rows(x, g):
    ms = jnp.mean(x * x, axis=-1, keepdims=True)
    return x * lax.rsqrt(ms + RMS_EPS) * g


def _layernorm_rows(x, g, b):
    mu = jnp.mean(x, axis=-1, keepdims=True)
    xc = x - mu
    var = jnp.mean(xc * xc, axis=-1, keepdims=True)
    return xc * lax.rsqrt(var + LN_EPS) * g + b


def _load_rows(x_ref, r0, n):
    if len(x_ref.shape) == 2:
        return x_ref[r0:r0 + n, :]
    return x_ref[r0 // NSEQ:(r0 + n) // NSEQ, :, :].reshape(n, x_ref.shape[2])


def _norm_in(x_ref, g_ref, h_scr, tile):
    rc = 32
    for r0 in range(0, tile, rc):
        h_scr[r0:r0 + rc, :] = _rmsnorm_rows(_load_rows(x_ref, r0, rc), g_ref[...]).astype(BF16)


def _out_proj(xres_ref, m_scr, wout_ref, fg_ref, xo_ref, tile):
    out = jnp.dot(m_scr[...], wout_ref[...], preferred_element_type=F32)
    xn = _load_rows(xres_ref, 0, tile) + out
    if fg_ref is not None:
        xn = _rmsnorm_rows(xn, fg_ref[...])
    xo_ref[...] = xn.reshape(xo_ref.shape)


def _zero_at_start(*scratch):
    @pl.when(pl.program_id(0) == 0)
    def _():
        for ref in scratch:
            ref[...] = jnp.zeros(ref.shape, ref.dtype)


def _pool_kernel(*refs, tile, tps, n_tiles, stride, has_state, has_final):
    refs = list(refs)
    x_ref, xres_ref = refs[:2]
    refs = refs[2:]
    state_ref = refs.pop(0) if has_state else None
    g_ref, win_ref, wgrp_ref, scale_ref, inv_ref, wout_ref = refs[:6]
    refs = refs[6:]
    fg_ref = refs.pop(0) if has_final else None
    xo_ref, st_ref, h_scr, ext_scr, z_scr, sg_scr, d_scr, m_scr = refs

    e = z_scr.shape[1]
    grp = e // len(POOL_WINDOWS)
    base = POOL_PAD * stride
    hist = POOL_HIST * stride
    rc = 32 if stride == NSEQ else 64

    _zero_at_start(ext_scr, z_scr)

    for r0 in range(0, tile, 32):
        sg_scr[r0:r0 + 32, :] = scale_ref[...] * _silu(z_scr[r0:r0 + 32, :])

    _norm_in(x_ref, g_ref, h_scr, tile)
    h = h_scr[...]

    prev = pl.program_id(0) - 1
    sel = jnp.where(prev % tps == 0, 0, 1)
    opens_group = jnp.minimum(pl.program_id(0), n_tiles - 1) % tps == 0

    for gi, w in enumerate(POOL_WINDOWS):
        cols = slice(gi * grp, (gi + 1) * grp)
        for r0 in range(0, tile, rc):
            if stride == NSEQ:
                cur = ext_scr[base + r0:base + r0 + rc, cols]
                acc = cur
                for k in range(1, w):
                    acc = acc + ext_scr[base + r0 - k * NSEQ:base + r0 - k * NSEQ + rc, cols]
            else:
                t = ext_scr[r0:r0 + base + rc, cols]
                cur = t[base:, :]
                sh = 1
                while sh < w:
                    t = t + pltpu.roll(t, sh, 0)
                    sh *= 2
                acc = t[base:, :]
            if r0 < base:
                nt = min(rc, base - r0)
                inv = jnp.tile(inv_ref[sel, r0:r0 + nt, gi * LANES:(gi + 1) * LANES], (1, grp // LANES))
                if nt < rc:
                    inv = jnp.concatenate([inv, jnp.full((rc - nt, grp), 1.0 / w, F32)], axis=0)
                mean = acc * inv
            else:
                mean = acc * (1.0 / w)
            d_scr[r0:r0 + rc, cols] = (mean - cur).astype(BF16)

        if state_ref is not None:
            ext_scr[base - hist:base, cols] = state_ref[:, cols]
        else:
            tail = ext_scr[tile:tile + base, cols]
            ext_scr[0:base, cols] = jnp.where(opens_group, jnp.zeros_like(tail), tail)
        u = jnp.dot(h, win_ref[:, cols], preferred_element_type=F32)
        ext_scr[base:base + tile, cols] = u
        st_ref[:, cols] = u[tile - hist:, :]

        y = jnp.dot(d_scr[:, cols], wgrp_ref[gi], preferred_element_type=F32)
        m_scr[:, cols] = (y * sg_scr[:, cols]).astype(BF16)

    z_scr[...] = jnp.dot(h, win_ref[:, e:2 * e], preferred_element_type=F32)

    _out_proj(xres_ref, m_scr, wout_ref, fg_ref, xo_ref, tile)


def _conv_kernel(*refs, tile, tps, has_state, has_final):
    refs = list(refs)
    x_ref, xres_ref = refs[:2]
    refs = refs[2:]
    state_ref = refs.pop(0) if has_state else None
    g_ref, win_ref, cw_ref, cb_ref, ng_ref, nb_ref, wout_ref = refs[:7]
    refs = refs[7:]
    fg_ref = refs.pop(0) if has_final else None
    xo_ref, st_ref, h_scr, ext_scr, sz_scr, y_scr, m_scr = refs

    e = sz_scr.shape[1]
    base = CONV_PAD * NSEQ
    hist = CONV_HIST * NSEQ
    first = base - hist
    cblk = 512
    rc = CONV_ROW_BLOCK
    nq = rc // NSEQ

    _zero_at_start(y_scr, sz_scr)

    if state_ref is None:
        @pl.when(pl.program_id(0) % tps == 0)
        def _():
            ext_scr[0:base, :] = jnp.zeros((base, e), F32)
    else:
        ext_scr[first:base, :] = state_ref[...]

    _norm_in(x_ref, g_ref, h_scr, tile)

    for r0 in range(0, tile, 16):
        yn = _layernorm_rows(y_scr[r0:r0 + 16, :], ng_ref[...], nb_ref[...])
        m_scr[r0:r0 + 16, :] = (_silu(yn) * sz_scr[r0:r0 + 16, :]).astype(BF16)

    h = h_scr[...]
    for c in range(e // cblk):
        cols = slice(c * cblk, (c + 1) * cblk)
        a = jnp.dot(h, win_ref[:, c * cblk:(c + 1) * cblk], preferred_element_type=F32)
        gl = jnp.dot(h, win_ref[:, e + c * cblk:e + (c + 1) * cblk], preferred_element_type=F32)
        ext_scr[base:base + tile, cols] = a * jax.nn.sigmoid(gl)
        sz_scr[:, cols] = _silu(jnp.dot(h, win_ref[:, 2 * e + c * cblk:2 * e + (c + 1) * cblk],
                                        preferred_element_type=F32))

    _out_proj(xres_ref, m_scr, wout_ref, fg_ref, xo_ref, tile)

    def conv_rows(i, carry):
        r0 = pl.multiple_of(i * rc, rc)
        for c in range(e // cblk):
            cols = slice(c * cblk, (c + 1) * cblk)
            acc = [cb_ref[:, cols] for _ in range(nq)]
            for k in range(CONV_WIDTH):
                wk = cw_ref[k, :, cols]
                for q in range(nq):
                    row = first + r0 + (k + q) * NSEQ
                    acc[q] = acc[q] + wk * ext_scr[pl.ds(row, NSEQ), cols]
            for q in range(nq):
                y_scr[pl.ds(r0 + q * NSEQ, NSEQ), cols] = acc[q]
        return carry

    lax.fori_loop(0, tile // rc, conv_rows, 0)

    st_ref[...] = ext_scr[base + tile - hist:base + tile, :]
    if state_ref is None:
        ext_scr[0:base, :] = ext_scr[tile:tile + base, :]


def _sgu_kernel(*refs, tile, tps, has_final):
    del tps
    refs = list(refs)
    x_ref, xres_ref, g_ref, win_ref, ng_ref, nb_ref, ws_ref, bs_ref, wout_ref = refs[:9]
    refs = refs[9:]
    fg_ref = refs.pop(0) if has_final else None
    xo_ref, st_ref, h_scr, u_scr, v_scr, z_scr, uz_scr, vb_scr, m_scr = refs

    e = z_scr.shape[1]
    hd = e // SGU_HEADS
    chunk = ws_ref.shape[1]
    st_rows = st_ref.shape[0]
    rn = 16

    _zero_at_start(u_scr, v_scr, z_scr, m_scr)

    _out_proj(xres_ref, m_scr, wout_ref, fg_ref, xo_ref, tile)

    for r0 in range(0, tile, rn):
        uz_scr[r0:r0 + rn, :] = u_scr[r0:r0 + rn, :] * _silu(z_scr[r0:r0 + rn, :])
        vn = _layernorm_rows(v_scr[r0:r0 + rn, :], ng_ref[...], nb_ref[...])
        vb_scr[r0:r0 + rn, :] = vn.astype(BF16)
        if r0 >= tile - st_rows:
            st_ref[r0 - (tile - st_rows):r0 - (tile - st_rows) + rn, :] = vn

    for c in range(tile // chunk):
        rows = slice(c * chunk, (c + 1) * chunk)
        for hh in range(SGU_HEADS):
            cols = slice(hh * hd, (hh + 1) * hd)
            sp = jnp.dot(ws_ref[hh], vb_scr[rows, cols], preferred_element_type=F32)
            sp = sp + jnp.tile(bs_ref[hh], (1, hd // bs_ref.shape[2]))
            m_scr[rows, cols] = (uz_scr[rows, cols] * sp).astype(BF16)

    _norm_in(x_ref, g_ref, h_scr, tile)
    h = h_scr[...]
    u_scr[...] = jnp.dot(h, win_ref[:, 0:e], preferred_element_type=F32)
    v_scr[...] = jnp.dot(h, win_ref[:, e:2 * e], preferred_element_type=F32)
    z_scr[...] = jnp.dot(h, win_ref[:, 2 * e:3 * e], preferred_element_type=F32)


def _resident(shape):
    nd = len(shape)
    return pl.BlockSpec(shape, lambda s: (0,) * nd, pipeline_mode=pl.Buffered(1))


def _params(vmem_limit):
    return pltpu.CompilerParams(dimension_semantics=("arbitrary",), vmem_limit_bytes=vmem_limit)


def _layer_call(kernel_fn, x, x_block, tile_index, n_tiles, tps, state, weights, final_g, st_shape,
                st_stage, scratch, name, lag=PIPE_LAG, out_x=None, vmem_limit=VMEM_LIMIT_BYTES):
    out_shape_x, out_block, out_index = out_x or (x.shape, x_block, tile_index)
    last = n_tiles - 1
    clamp = lambda t: jnp.clip(t, 0, last)
    operands = [x, x]
    in_specs = [pl.BlockSpec(x_block, lambda s: tile_index(clamp(s))),
                pl.BlockSpec(x_block, lambda s: tile_index(clamp(s - lag)))]
    if state is not None:
        operands.append(state)
        in_specs.append(pl.BlockSpec((None,) + state.shape[1:],
                                     lambda s: (clamp(s) // tps,) + (0,) * (state.ndim - 1)))
    for w in weights:
        operands.append(w)
        in_specs.append(_resident(w.shape))
    if final_g is not None:
        operands.append(final_g)
        in_specs.append(_resident(final_g.shape))
    out_shape = (jax.ShapeDtypeStruct(out_shape_x, F32), jax.ShapeDtypeStruct(st_shape, F32))
    out_specs = (pl.BlockSpec(out_block, lambda s: out_index(clamp(s - lag))),
                 pl.BlockSpec((None,) + st_shape[1:],
                              lambda s: (clamp(s - st_stage) // tps,) + (0,) * (len(st_shape) - 1)))
    return pl.pallas_call(
        functools.partial(kernel_fn, tps=tps),
        grid=(n_tiles + lag,),
        in_specs=in_specs,
        out_specs=out_specs,
        out_shape=out_shape,
        scratch_shapes=scratch,
        compiler_params=_params(vmem_limit),
        name=name,
    )(*operands)


def _group_geometry(x, state, tile_steps):
    steps, nseq, d = x.shape
    tsteps = min(tile_steps, steps)
    tps, groups = steps // tsteps, nseq // NSEQ
    assert state is not None or groups == 1
    assert state is None or tps == 1
    return tsteps * NSEQ, (tsteps, NSEQ, d), (lambda t: (t % tps, t // tps, 0)), tps * groups, tps


def _seq_tiles(layout, steps, nseq, d, tile):
    tps = steps // tile
    if layout == "seq":
        return (nseq, steps, d), (None, tile, d), (lambda t: (t // tps, t % tps, 0))
    return (steps, nseq * d), (tile, d), (lambda t: (t % tps, t // tps))


def _pool_counts(offset, stride):
    pos = offset + jnp.arange(POOL_PAD * stride) // stride
    w = jnp.repeat(jnp.asarray(POOL_WINDOWS), LANES)
    start = 1.0 / jnp.minimum(pos[:, None] + 1, w[None, :]).astype(F32)
    steady = jnp.broadcast_to(1.0 / w.astype(F32), start.shape)
    return jnp.stack([start, steady])


def _pool_layer(x, state, norm_g, in_w, grp_w, scale, out_w, final_g, *, offset, name, layouts=None):
    e = out_w.shape[0]
    if layouts is None:
        d = x.shape[2]
        tile, x_block, tile_index, n_tiles, tps = _group_geometry(x, state, POOL_TILE_STEPS)
        stride, out_x = NSEQ, None
    else:
        nseq, steps, d = x.shape if layouts[0] == "seq" else (x.shape[1], x.shape[0], x.shape[2])
        tile = min(POOL_SEQ_TILE, steps)
        shape, x_block, tile_index = _seq_tiles(layouts[0], steps, nseq, d, tile)
        x = x.reshape(shape)
        out_x = _seq_tiles(layouts[1], steps, nseq, d, tile)
        tps, stride = steps // tile, 1
        n_tiles = tps * nseq
    kern = functools.partial(_pool_kernel, tile=tile, n_tiles=n_tiles, stride=stride,
                             has_state=state is not None, has_final=final_g is not None)
    scratch = [pltpu.VMEM((tile, d), BF16), pltpu.VMEM((POOL_PAD * stride + tile, e), F32),
               pltpu.VMEM((tile, e), F32), pltpu.VMEM((tile, e), F32), pltpu.VMEM((tile, e), BF16),
               pltpu.VMEM((tile, e), BF16)]
    weights = [norm_g, in_w, grp_w, scale, _pool_counts(offset, stride), out_w]
    xo, st = _layer_call(kern, x, x_block, tile_index, n_tiles, tps, state, weights, final_g,
                         (n_tiles // tps, POOL_HIST * stride, e), 0, scratch, name, out_x=out_x, lag=1)
    if layouts is not None and layouts[1] == "time":
        xo = xo.reshape(steps, nseq, d)
    return xo, st


def _conv_layer(x, state, norm_g, in_w, cw, cb, ng, nb, out_w, final_g, *, name):
    d = x.shape[2]
    e = out_w.shape[0]
    tile, x_block, tile_index, n_tiles, tps = _group_geometry(x, state, CONV_TILE_STEPS)
    kern = functools.partial(_conv_kernel, tile=tile, has_state=state is not None,
                             has_final=final_g is not None)
    scratch = [pltpu.VMEM((tile, d), BF16), pltpu.VMEM((CONV_PAD * NSEQ + tile, e), F32),
               pltpu.VMEM((tile, e), F32), pltpu.VMEM((tile, e), F32), pltpu.VMEM((tile, e), BF16)]
    return _layer_call(kern, x, x_block, tile_index, n_tiles, tps, state,
                       [norm_g, in_w, cw, cb, ng, nb, out_w], final_g,
                       (n_tiles // tps, CONV_HIST * NSEQ, e), 0, scratch, name, lag=1,
                       vmem_limit=CONV_VMEM_LIMIT_BYTES)


def _sgu_layer(x, norm_g, in_w, ng, nb, ws, bs, out_w, final_g, *, name):
    steps, nseq, d = x.shape
    e = out_w.shape[0]
    if steps >= SGU_CHUNK:
        tile = SGU_TILE
        tps, groups = steps // tile, nseq
        xk = x.reshape(steps, nseq * d)
        x_block, tile_index = (tile, d), (lambda t: (t % tps, t // tps))
        wm = jnp.where(jnp.tril(jnp.ones((SGU_CHUNK, SGU_CHUNK), dtype=bool))[None], ws, jnp.zeros_like(ws))
        bias = bs
        st_rows = SGU_CHUNK
    else:
        tile = steps * NSEQ
        tps, groups = 1, nseq // NSEQ
        xk = x
        x_block, tile_index = (steps, NSEQ, d), (lambda t: (0, t, 0))
        wl = ws[:, :steps, :steps]
        wm = jnp.where(jnp.tril(jnp.ones((steps, steps), dtype=bool))[None], wl, jnp.zeros_like(wl))
        wm = jnp.einsum('hij,ab->hiajb', wm, jnp.eye(NSEQ, dtype=ws.dtype)).reshape(ws.shape[0], tile, tile)
        bias = jnp.repeat(bs[:, :steps], NSEQ, axis=1)
        st_rows = tile
    bias = jnp.broadcast_to(bias[:, :, None], bias.shape + (LANES,))
    kern = functools.partial(_sgu_kernel, tile=tile, has_final=final_g is not None)
    scratch = [pltpu.VMEM((tile, d), BF16), pltpu.VMEM((tile, e), F32), pltpu.VMEM((tile, e), F32),
               pltpu.VMEM((tile, e), F32), pltpu.VMEM((tile, e), F32), pltpu.VMEM((tile, e), BF16),
               pltpu.VMEM((tile, e), BF16)]
    xo, st = _layer_call(kern, xk, x_block, tile_index, tps * groups, tps, None,
                         [norm_g, in_w, ng, nb, wm.astype(BF16), bias, out_w], final_g,
                         (groups, st_rows, e), 1, scratch, name)
    return xo.reshape(steps, nseq, d), st


def _to_step_major(state):
    n, s, e = state.shape
    return state.reshape(n // NSEQ, NSEQ, s, e).transpose(0, 2, 1, 3).reshape(n // NSEQ, s * NSEQ, e)


def _to_seq_major(state, steps):
    g, _, e = state.shape
    return state.reshape(g, steps, NSEQ, e).transpose(0, 2, 1, 3).reshape(g * NSEQ, steps, e)


def _run_group(x, state_pool, state_conv, offset, name, norm_g, pool_in_w, pool_w, pool_scale,
               pool_out_w, conv_in_w, conv_w, conv_b, conv_norm_g, conv_norm_b, conv_out_w,
               sgu_in_w, sgu_norm_g, sgu_norm_b, sgu_w, sgu_b, sgu_out_w, final_g):
    n_seq, seq_len, d = x.shape
    depth = norm_g.shape[0]
    seq_pool = state_pool is None and seq_len >= POOL_SEQ_TILE
    seq_in = seq_pool and depth > 0
    seq_out = seq_pool and (depth - 1) % 3 == 0
    xt = x if seq_in else x.transpose(1, 0, 2)
    row = lambda v: v.reshape(1, -1)
    new_pool, new_conv, new_sgu = [], [], []
    for i in range(depth):
        kind, j = i % 3, i // 3
        fg = row(final_g) if i == depth - 1 else None
        lname = f"{name}_l{i}"
        if kind == 0 and seq_pool:
            layouts = ("seq" if i == 0 else "time", "seq" if i == depth - 1 else "time")
            xt, s = _pool_layer(xt, None, row(norm_g[i]), pool_in_w[j], pool_w[j], row(pool_scale[j]),
                                pool_out_w[j], fg, offset=offset, name=lname, layouts=layouts)
            new_pool.append(s)
        elif kind == 0:
            st = None if state_pool is None else _to_step_major(state_pool[j])
            xt, s = _pool_layer(xt, st, row(norm_g[i]), pool_in_w[j], pool_w[j], row(pool_scale[j]),
                                pool_out_w[j], fg, offset=offset, name=lname)
            new_pool.append(_to_seq_major(s, POOL_HIST))
        elif kind == 1:
            st = None if state_conv is None else _to_step_major(state_conv[j])
            cw = jnp.broadcast_to(conv_w[j][:, None, :], (CONV_WIDTH, NSEQ, conv_w.shape[2]))
            cb = jnp.broadcast_to(conv_b[j][None, :], (NSEQ, conv_b.shape[1]))
            xt, s = _conv_layer(xt, st, row(norm_g[i]), conv_in_w[j], cw, cb, row(conv_norm_g[j]),
                                row(conv_norm_b[j]), conv_out_w[j], fg, name=lname)
            new_conv.append(_to_seq_major(s, CONV_HIST))
        else:
            xt, s = _sgu_layer(xt, row(norm_g[i]), sgu_in_w[j], row(sgu_norm_g[j]), row(sgu_norm_b[j]),
                               sgu_w[j], sgu_b[j], sgu_out_w[j], fg, name=lname)
            new_sgu.append(s if seq_len >= SGU_CHUNK else _to_seq_major(s, seq_len))
    y = xt if seq_out else xt.transpose(1, 0, 2)
    return y, jnp.stack(new_pool), jnp.stack(new_conv), jnp.stack(new_sgu)


def kernel(x_prompt, x_sample, state_pool, state_conv, norm_g, pool_in_w, pool_w, pool_scale, pool_out_w,
           conv_in_w, conv_w, conv_b, conv_norm_g, conv_norm_b, conv_out_w,
           sgu_in_w, sgu_norm_g, sgu_norm_b, sgu_w, sgu_b, sgu_out_w, final_g):
    past_len = 4096
    weights = (norm_g, pool_in_w.astype(BF16), pool_w.astype(BF16), pool_scale, pool_out_w.astype(BF16),
               conv_in_w.astype(BF16), conv_w, conv_b, conv_norm_g, conv_norm_b, conv_out_w.astype(BF16),
               sgu_in_w.astype(BF16), sgu_norm_g, sgu_norm_b, sgu_w, sgu_b, sgu_out_w.astype(BF16), final_g)
    y_p, pool_p, conv_p, sgu_p = _run_group(x_prompt, None, None, 0, "prompt", *weights)
    y_s, pool_s, conv_s, sgu_s = _run_group(x_sample, state_pool, state_conv, past_len, "sample", *weights)
    return (y_p, y_s, pool_p, pool_s, conv_p, conv_s, sgu_p, sgu_s)
```
